```python
import jax, jax.numpy as jnp
from jax import lax
import numpy as np

D_MODEL = 1024
BATCH = 8
SEQ = 8192
DEPTH = 1
DEC_BATCH = 2
DEC_SEQ = 8192
PAST_LEN = 128

N_HEADS = 8
N_KV_HEADS = 2
HEAD_DIM = 64
GROUP = N_HEADS // N_KV_HEADS
ATTN_WIDTH = N_HEADS * HEAD_DIM
KV_WIDTH = N_KV_HEADS * HEAD_DIM
WINDOW = 128
BLOCK = 128
ROPE_THETA = 10000.0
POOL_WINDOWS = (2, 4, 8, 16)
N_POOL_GROUPS = 4
POOL_GROUP_DIM = 128
POOL_WIDTH = N_POOL_GROUPS * POOL_GROUP_DIM
N_BRANCHES = 2
IN_WIDTH = ATTN_WIDTH + 2 * KV_WIDTH + POOL_WIDTH + N_BRANCHES * D_MODEL
D_FF = 2816
N_SUBLAYERS = 3
FFN_RES_WEIGHT = 0.5
EPS = 1e-6
NEG_INF = -1e30

kernel_name = "hybrid_swa_pool_macaron_encoder"


def rmsnorm(x, g):
    xf = x.astype(jnp.float32)
    y = xf * lax.rsqrt(jnp.mean(xf * xf, axis=-1, keepdims=True) + EPS)
    return (y * g.astype(jnp.float32)).astype(x.dtype)


def rope(x, pos):
    half = HEAD_DIM // 2
    inv_freq = ROPE_THETA ** (-jnp.arange(half, dtype=jnp.float32) / half)
    ang = pos.astype(jnp.float32)[:, None] * inv_freq[None, :]
    cos = jnp.cos(ang)[None, :, None, :]
    sin = jnp.sin(ang)[None, :, None, :]
    xf = x.astype(jnp.float32)
    x1, x2 = xf[..., :half], xf[..., half:]
    out = jnp.concatenate([x1 * cos - x2 * sin, x2 * cos + x1 * sin], axis=-1)
    return out.astype(x.dtype)


def windowed_attention(q, k, v, sink):
    B, S = q.shape[0], q.shape[1]
    nb = S // BLOCK
    qb = jnp.moveaxis(q.reshape(B, nb, BLOCK, N_KV_HEADS, GROUP, HEAD_DIM), 1, 0)
    kp = jnp.pad(k, ((0, 0), (BLOCK, BLOCK), (0, 0), (0, 0)))
    vp = jnp.pad(v, ((0, 0), (BLOCK, BLOCK), (0, 0), (0, 0)))
    scale = HEAD_DIM ** -0.5
    r = jnp.arange(BLOCK)
    jj = jnp.arange(3 * BLOCK)
    sink_g = sink.astype(jnp.float32).reshape(N_KV_HEADS, GROUP)[None, :, :, None, None]

    def block_fn(args):
        i, qi = args
        ki = lax.dynamic_slice_in_dim(kp, i * BLOCK, 3 * BLOCK, axis=1)
        vi = lax.dynamic_slice_in_dim(vp, i * BLOCK, 3 * BLOCK, axis=1)
        s = jnp.einsum('bqkgd,bnkd->bkgqn', qi, ki,
                       preferred_element_type=jnp.float32) * scale
        qpos = i * BLOCK + r
        kpos = (i - 1) * BLOCK + jj
        valid = (jnp.abs(qpos[:, None] - kpos[None, :]) <= WINDOW) & (kpos[None, :] >= 0) & (kpos[None, :] < S)
        s = jnp.where(valid, s, NEG_INF)
        m = jnp.maximum(jnp.max(s, axis=-1, keepdims=True), sink_g)
        p = jnp.exp(s - m)
        p = p / (jnp.sum(p, axis=-1, keepdims=True) + jnp.exp(sink_g - m))
        return jnp.einsum('bkgqn,bnkd->bqkgd', p.astype(vi.dtype), vi)

    out = lax.map(block_fn, (jnp.arange(nb), qb))
    return jnp.moveaxis(out, 0, 1).reshape(B, S, ATTN_WIDTH)


def multiscale_pool(u, pool_w, pool_scale):
    B, S, _ = u.shape
    uf = u.astype(jnp.float32)
    cs = jnp.pad(jnp.cumsum(uf, axis=1), ((0, 0), (1, 0), (0, 0)))
    t = jnp.arange(S)
    outs = []
    for g, w in enumerate(POOL_WINDOWS):
        lo = jnp.clip(t - w // 2, 0, S)
        hi = jnp.clip(t + w // 2, 0, S)
        csg = cs[..., g * POOL_GROUP_DIM:(g + 1) * POOL_GROUP_DIM]
        cnt = (hi - lo).astype(jnp.float32)[None, :, None]
        mean = (csg[:, hi] - csg[:, lo]) / cnt
        outs.append(mean - uf[..., g * POOL_GROUP_DIM:(g + 1) * POOL_GROUP_DIM])
    pooled = jnp.stack(outs, axis=2).astype(u.dtype)
    mixed = jnp.einsum('bsgc,gcd->bsgd', pooled, pool_w).reshape(B, S, POOL_WIDTH)
    return mixed * pool_scale


def swiglu(h, w_gate, w_up, w_down):
    return (jax.nn.silu(h @ w_gate) * (h @ w_up)) @ w_down


def token_mixer(h, w_in, attn_sink, w_attn_branch, pool_w, pool_scale, w_pool_branch, w_out):
    B, S, _ = h.shape
    proj = h @ w_in
    e_q = ATTN_WIDTH
    e_k = e_q + KV_WIDTH
    e_v = e_k + KV_WIDTH
    e_u = e_v + POOL_WIDTH
    q = proj[..., :e_q].reshape(B, S, N_HEADS, HEAD_DIM)
    k = proj[..., e_q:e_k].reshape(B, S, N_KV_HEADS, HEAD_DIM)
    v = proj[..., e_k:e_v].reshape(B, S, N_KV_HEADS, HEAD_DIM)
    u = proj[..., e_v:e_u]
    g_a = proj[..., e_u:e_u + D_MODEL]
    g_b = proj[..., e_u + D_MODEL:]
    pos = jnp.arange(S)
    q = rope(q, pos)
    k = rope(k, pos)
    a = windowed_attention(q, k, v, attn_sink) @ w_attn_branch
    p = multiscale_pool(u, pool_w, pool_scale) @ w_pool_branch
    merged = jax.nn.sigmoid(g_a) * a + jax.nn.sigmoid(g_b) * p
    return merged @ w_out


def encoder_layer(x, c, ada_w, ada_b, norm_pre, norm_post,
                  ffn1_w_gate, ffn1_w_up, ffn1_w_down,
                  w_in, attn_sink, w_attn_branch, pool_w, pool_scale, w_pool_branch, w_out,
                  ffn2_w_gate, ffn2_w_up, ffn2_w_down):
    B = x.shape[0]
    mod = (jax.nn.silu(c) @ ada_w + ada_b).reshape(B, N_SUBLAYERS, 3, D_MODEL)

    def sublayer(x, j, fn, res_w):
        shift = mod[:, j, 0][:, None, :]
        scl = mod[:, j, 1][:, None, :]
        gate = mod[:, j, 2][:, None, :]
        h = rmsnorm(x, norm_pre[j]) * (1.0 + scl) + shift
        y = rmsnorm(fn(h), norm_post[j])
        return x + res_w * gate * y

    x = sublayer(x, 0, lambda h: swiglu(h, ffn1_w_gate, ffn1_w_up, ffn1_w_down), FFN_RES_WEIGHT)
    x = sublayer(x, 1, lambda h: token_mixer(h, w_in, attn_sink, w_attn_branch, pool_w,
                                             pool_scale, w_pool_branch, w_out), 1.0)
    x = sublayer(x, 2, lambda h: swiglu(h, ffn2_w_gate, ffn2_w_up, ffn2_w_down), FFN_RES_WEIGHT)
    return x


def setup_inputs(seed: int = 0) -> dict:
    key = jax.random.key(seed)
    ks = jax.random.split(key, 24)
    f32 = jnp.float32

    def nrm(k, shape, s):
        return jax.random.normal(k, shape, f32) * s

    L = DEPTH
    return {
        "x_prompt": nrm(ks[0], (BATCH, SEQ, D_MODEL), 1.0),
        "x_sample": nrm(ks[1], (DEC_BATCH, DEC_SEQ, D_MODEL), 1.0),
        "c_prompt": nrm(ks[2], (BATCH, D_MODEL), 1.0),
        "c_sample": nrm(ks[3], (DEC_BATCH, D_MODEL), 1.0),
        "ada_w": nrm(ks[4], (L, D_MODEL, N_SUBLAYERS * 3 * D_MODEL), 0.5 * D_MODEL ** -0.5),
        "ada_b": nrm(ks[5], (L, N_SUBLAYERS * 3 * D_MODEL), 0.01),
        "norm_pre": 1.0 + nrm(ks[6], (L, N_SUBLAYERS, D_MODEL), 0.05),
        "norm_post": 1.0 + nrm(ks[7], (L, N_SUBLAYERS, D_MODEL), 0.05),
        "ffn1_w_gate": nrm(ks[8], (L, D_MODEL, D_FF), D_MODEL ** -0.5),
        "ffn1_w_up": nrm(ks[9], (L, D_MODEL, D_FF), D_MODEL ** -0.5),
        "ffn1_w_down": nrm(ks[10], (L, D_FF, D_MODEL), D_FF ** -0.5),
        "w_in": nrm(ks[11], (L, D_MODEL, IN_WIDTH), D_MODEL ** -0.5),
        "attn_sink": nrm(ks[12], (L, N_HEADS), 1.0),
        "w_attn_branch": nrm(ks[13], (L, ATTN_WIDTH, D_MODEL), ATTN_WIDTH ** -0.5),
        "pool_w": nrm(ks[14], (L, N_POOL_GROUPS, POOL_GROUP_DIM, POOL_GROUP_DIM), POOL_GROUP_DIM ** -0.5),
        "pool_scale": 1.0 + nrm(ks[15], (L, POOL_WIDTH), 0.1),
        "w_pool_branch": nrm(ks[16], (L, POOL_WIDTH, D_MODEL), POOL_WIDTH ** -0.5),
        "w_out": nrm(ks[17], (L, D_MODEL, D_MODEL), D_MODEL ** -0.5),
        "ffn2_w_gate": nrm(ks[18], (L, D_MODEL, D_FF), D_MODEL ** -0.5),
        "ffn2_w_up": nrm(ks[19], (L, D_MODEL, D_FF), D_MODEL ** -0.5),
        "ffn2_w_down": nrm(ks[20], (L, D_FF, D_MODEL), D_FF ** -0.5),
    }


def reference(x_prompt, x_sample, c_prompt, c_sample, ada_w, ada_b, norm_pre, norm_post,
              ffn1_w_gate, ffn1_w_up, ffn1_w_down, w_in, attn_sink, w_attn_branch, pool_w,
              pool_scale, w_pool_branch, w_out, ffn2_w_gate, ffn2_w_up, ffn2_w_down):
    y_prompt = x_prompt
    y_sample = x_sample
    for l in range(DEPTH):
        lp = (ada_w[l], ada_b[l], norm_pre[l], norm_post[l],
              ffn1_w_gate[l], ffn1_w_up[l], ffn1_w_down[l],
              w_in[l], attn_sink[l], w_attn_branch[l], pool_w[l], pool_scale[l],
              w_pool_branch[l], w_out[l],
              ffn2_w_gate[l], ffn2_w_up[l], ffn2_w_down[l])
        y_prompt = encoder_layer(y_prompt, c_prompt, *lp)
        y_sample = encoder_layer(y_sample, c_sample, *lp)
    return (y_prompt, y_sample)
```

```python
import functools

import jax
import jax.numpy as jnp
from jax import lax
from jax.experimental import pallas as pl
from jax.experimental.pallas import tpu as pltpu

D_MODEL = 1024
N_HEADS = 8
N_KV_HEADS = 2
HEAD_DIM = 64
GROUP = N_HEADS // N_KV_HEADS
ATTN_WIDTH = N_HEADS * HEAD_DIM
KV_WIDTH = N_KV_HEADS * HEAD_DIM
WINDOW = 128
ROPE_THETA = 10000.0
POOL_WINDOWS = (2, 4, 8, 16)
POOL_GROUP_DIM = 128
POOL_WIDTH = len(POOL_WINDOWS) * POOL_GROUP_DIM
D_FF = 2816
N_SUBLAYERS = 3
FFN_RES_WEIGHT = 0.5
EPS = 1e-6
NEG_INF = -1e30

V7X_LANES = 128
V7X_VMEM_BYTES = 64 * 1024 * 1024

HALO = WINDOW
FFN_TILE = 512
MIXER_TILE = 512
ADA_COLS = 1536
VMEM_LIMIT = V7X_VMEM_BYTES - 8 * 1024 * 1024

F32 = jnp.float32
BF16 = jnp.bfloat16


def _sigmoid(x):
    return 1.0 / (1.0 + jnp.exp(-x))


def _rmsnorm(x, g):
    return x * lax.rsqrt(jnp.mean(x * x, axis=-1, keepdims=True) + EPS) * g


def _resident(shape):
    return pl.BlockSpec(shape, lambda *_: (0,) * len(shape), pipeline_mode=pl.Buffered(1))


def _ada_kernel(c_ref, w_ref, b_ref, o_ref):
    c = c_ref[...]
    s = c * _sigmoid(c)
    o_ref[...] = jnp.dot(s, w_ref[...], preferred_element_type=F32,
                         precision=lax.Precision.HIGHEST) + b_ref[...]


def _ada_mod(c, ada_w, ada_b):
    rows, width = c.shape[0], ada_w.shape[1]
    return pl.pallas_call(
        _ada_kernel,
        grid=(width // ADA_COLS,),
        in_specs=[
            pl.BlockSpec((rows, D_MODEL), lambda n: (0, 0)),
            pl.BlockSpec((D_MODEL, ADA_COLS), lambda n: (0, n)),
            pl.BlockSpec((1, ADA_COLS), lambda n: (0, n)),
        ],
        out_specs=pl.BlockSpec((rows, ADA_COLS), lambda n: (0, n)),
        out_shape=jax.ShapeDtypeStruct((rows, width), F32),
        compiler_params=pltpu.CompilerParams(dimension_semantics=("arbitrary",),
                                             vmem_limit_bytes=VMEM_LIMIT),
        name="ada_mod",
    )(c, ada_w, ada_b.reshape(1, width))


def _ffn_kernel(x_ref, mod_ref, gpre_ref, gpost_ref, wg_ref, wu_ref, wd_ref, o_ref, *, sub):
    x = x_ref[0]
    shift = mod_ref[0, 3 * sub:3 * sub + 1, :]
    scl = mod_ref[0, 3 * sub + 1:3 * sub + 2, :]
    gate = mod_ref[0, 3 * sub + 2:3 * sub + 3, :]
    h = (_rmsnorm(x, gpre_ref[sub:sub + 1, :]) * (1.0 + scl) + shift).astype(BF16)
    g = jnp.dot(h, wg_ref[...], preferred_element_type=F32)
    u = jnp.dot(h, wu_ref[...], preferred_element_type=F32)
    a = (g * _sigmoid(g) * u).astype(BF16)
    y = jnp.dot(a, wd_ref[...], preferred_element_type=F32)
    y = _rmsnorm(y, gpost_ref[sub:sub + 1, :])
    o_ref[0] = x + (FFN_RES_WEIGHT * gate) * y


def _ffn_sublayer(x, mod, norm_pre, norm_post, wg, wu, wd, sub):
    batch, seq, _ = x.shape
    tile = min(FFN_TILE, seq)
    return pl.pallas_call(
        functools.partial(_ffn_kernel, sub=sub),
        grid=(batch, seq // tile),
        in_specs=[
            pl.BlockSpec((1, tile, D_MODEL), lambda b, i: (b, i, 0)),
            pl.BlockSpec((1, 3 * N_SUBLAYERS, D_MODEL), lambda b, i: (b, 0, 0)),
            _resident((N_SUBLAYERS, D_MODEL)),
            _resident((N_SUBLAYERS, D_MODEL)),
            _resident((D_MODEL, D_FF)),
            _resident((D_MODEL, D_FF)),
            _resident((D_FF, D_MODEL)),
        ],
        out_specs=pl.BlockSpec((1, tile, D_MODEL), lambda b, i: (b, i, 0)),
        out_shape=jax.ShapeDtypeStruct(x.shape, F32),
        compiler_params=pltpu.CompilerParams(dimension_semantics=("arbitrary", "arbitrary"),
                                             vmem_limit_bytes=VMEM_LIMIT),
        name=f"ffn_sublayer{sub}",
    )(x, mod, norm_pre, norm_post, wg, wu, wd)


def _rope(x, cos, sin_signed, first_half):
    partner = jnp.where(first_half, pltpu.roll(x, V7X_LANES - HEAD_DIM // 2, 1),
                        pltpu.roll(x, HEAD_DIM // 2, 1))
    return x * cos + partner * sin_signed


def _mixer_kernel(xm_ref, xl_ref, xr_ref, mod_ref, gpre_ref, gpost_ref, cos_ref, sin_ref, sink_ref,
                  wkvu_ref, wqg_ref, wab_ref, poolw_ref, pscale_ref, wpb_ref, wout_ref,
                  o_ref,
                  hext_ref, q_ref, k_ref, v_ref, u_ref, attn_ref, mixed_ref, *, sub, seq):
    tile = xm_ref.shape[1]
    ext = tile + 2 * HALO
    i = pl.program_id(1)
    n_tiles = pl.num_programs(1)
    shift = mod_ref[0, 3 * sub:3 * sub + 1, :]
    scl = mod_ref[0, 3 * sub + 1:3 * sub + 2, :]
    gate = mod_ref[0, 3 * sub + 2:3 * sub + 3, :]
    gpre = gpre_ref[sub:sub + 1, :]

    def pre(x):
        return (_rmsnorm(x, gpre) * (1.0 + scl) + shift).astype(BF16)

    x = xm_ref[0]
    hext_ref[0:HALO, :] = pre(xl_ref[0])
    hext_ref[HALO:HALO + tile, :] = pre(x)
    hext_ref[HALO + tile:ext, :] = pre(xr_ref[0])

    kvu = jnp.dot(hext_ref[...], wkvu_ref[...], preferred_element_type=F32)
    qg = jnp.dot(hext_ref[HALO:HALO + tile, :], wqg_ref[...], preferred_element_type=F32)

    cos = cos_ref[0]
    sin = sin_ref[0]
    lane = lax.broadcasted_iota(jnp.int32, (1, V7X_LANES), 1)
    first_half = (lane % HEAD_DIM) < (HEAD_DIM // 2)

    k = _rope(kvu[:, 0:KV_WIDTH], cos, sin, first_half).astype(BF16)
    v = kvu[:, KV_WIDTH:2 * KV_WIDTH].astype(BF16)
    for kh in range(N_KV_HEADS):
        k_ref[kh] = k[:, kh * HEAD_DIM:(kh + 1) * HEAD_DIM]
        v_ref[kh] = v[:, kh * HEAD_DIM:(kh + 1) * HEAD_DIM]

    cos_m = cos[HALO:HALO + tile, :]
    sin_m = sin[HALO:HALO + tile, :]
    for cb in range(ATTN_WIDTH // V7X_LANES):
        qc = _rope(qg[:, cb * V7X_LANES:(cb + 1) * V7X_LANES], cos_m, sin_m, first_half)
        qc = (qc * (HEAD_DIM ** -0.5)).astype(BF16)
        q_ref[2 * cb] = qc[:, 0:HEAD_DIM]
        q_ref[2 * cb + 1] = qc[:, HEAD_DIM:2 * HEAD_DIM]

    u_ext = kvu[:, 2 * KV_WIDTH:2 * KV_WIDTH + POOL_WIDTH]
    u_ref[0:HALO, :] = jnp.where(i > 0, u_ext[0:HALO, :], 0.0)
    u_ref[HALO:HALO + tile, :] = u_ext[HALO:HALO + tile, :]
    u_ref[HALO + tile:ext, :] = jnp.where(i < n_tiles - 1, u_ext[HALO + tile:ext, :], 0.0)

    blk = WINDOW
    for b in range(tile // blk):
        base = i * tile + b * blk - HALO
        r = lax.broadcasted_iota(jnp.int32, (blk, 3 * blk), 0)
        jj = lax.broadcasted_iota(jnp.int32, (blk, 3 * blk), 1)
        valid = (jj >= r) & (jj <= r + 2 * WINDOW) & (jj + base >= 0) & (jj + base < seq)
        for kh in range(N_KV_HEADS):
            q4 = q_ref[GROUP * kh:GROUP * (kh + 1), b * blk:(b + 1) * blk, :].reshape(GROUP * blk, HEAD_DIM)
            kw = k_ref[kh, b * blk:b * blk + 3 * blk, :]
            vw = v_ref[kh, b * blk:b * blk + 3 * blk, :]
            s = lax.dot_general(q4, kw, (((1,), (1,)), ((), ())), preferred_element_type=F32)
            ps, inv = [], []
            for gi in range(GROUP):
                sink = sink_ref[GROUP * kh + gi]
                sg = jnp.where(valid, s[gi * blk:(gi + 1) * blk, :], NEG_INF)
                m = jnp.maximum(jnp.max(sg, axis=-1, keepdims=True), sink)
                p = jnp.exp(sg - m)
                inv.append(1.0 / (jnp.sum(p, axis=-1, keepdims=True) + jnp.exp(sink - m)))
                ps.append(p.astype(BF16))
            o4 = jnp.dot(jnp.concatenate(ps, axis=0), vw, preferred_element_type=F32)
            for gi in range(GROUP):
                h = GROUP * kh + gi
                attn_ref[b * blk:(b + 1) * blk, h * HEAD_DIM:(h + 1) * HEAD_DIM] = (
                    o4[gi * blk:(gi + 1) * blk, :] * inv[gi]).astype(BF16)

    a = jnp.dot(attn_ref[...], wab_ref[...], preferred_element_type=F32)

    pos = i * tile + lax.broadcasted_iota(jnp.int32, (tile, 1), 0)
    for gi, w in enumerate(POOL_WINDOWS):
        cols = slice(gi * POOL_GROUP_DIM, (gi + 1) * POOL_GROUP_DIM)
        acc = u_ref[HALO - w // 2:HALO - w // 2 + tile, cols]
        for j in range(-w // 2 + 1, w // 2):
            acc = acc + u_ref[HALO + j:HALO + j + tile, cols]
        cnt = jnp.minimum(pos + w // 2, seq) - jnp.maximum(pos - w // 2, 0)
        pooled = acc / cnt.astype(F32) - u_ref[HALO:HALO + tile, cols]
        mixed = jnp.dot(pooled.astype(BF16), poolw_ref[gi], preferred_element_type=F32)
        mixed_ref[:, cols] = (mixed * pscale_ref[:, cols]).astype(BF16)
    p_branch = jnp.dot(mixed_ref[...], wpb_ref[...], preferred_element_type=F32)

    g_a = qg[:, ATTN_WIDTH:ATTN_WIDTH + D_MODEL]
    g_b = qg[:, ATTN_WIDTH + D_MODEL:ATTN_WIDTH + 2 * D_MODEL]
    merged = (_sigmoid(g_a) * a + _sigmoid(g_b) * p_branch).astype(BF16)
    y = jnp.dot(merged, wout_ref[...], preferred_element_type=F32)
    y = _rmsnorm(y, gpost_ref[sub:sub + 1, :])
    o_ref[0] = x + gate * y


def _rope_tables(seq, tile):
    half = HEAD_DIM // 2
    inv_freq = ROPE_THETA ** (-jnp.arange(half, dtype=F32) / half)
    lane = jnp.arange(V7X_LANES)
    pos = (jnp.arange(seq // tile) * tile - HALO)[:, None] + jnp.arange(tile + 2 * HALO)[None, :]
    ang = pos.astype(F32)[:, :, None] * inv_freq[lane % half][None, None, :]
    sign = jnp.where((lane % HEAD_DIM) < half, -1.0, 1.0).astype(F32)
    return jnp.cos(ang), jnp.sin(ang) * sign


def _mixer_sublayer(x, mod, norm_pre, norm_post, sink, wkvu, wqg, wab, poolw, pscale, wpb, wout, sub):
    batch, seq, _ = x.shape
    tile = min(MIXER_TILE, seq)
    ext = tile + 2 * HALO
    per_tile = tile // HALO
    last_halo = seq // HALO - 1
    cos, sin = _rope_tables(seq, tile)
    return pl.pallas_call(
        functools.partial(_mixer_kernel, sub=sub, seq=seq),
        grid=(batch, seq // tile),
        in_specs=[
            pl.BlockSpec((1, tile, D_MODEL), lambda b, i: (b, i, 0)),
            pl.BlockSpec((1, HALO, D_MODEL), lambda b, i: (b, jnp.maximum(i * per_tile - 1, 0), 0)),
            pl.BlockSpec((1, HALO, D_MODEL), lambda b, i: (b, jnp.minimum((i + 1) * per_tile, last_halo), 0)),
            pl.BlockSpec((1, 3 * N_SUBLAYERS, D_MODEL), lambda b, i: (b, 0, 0)),
            _resident((N_SUBLAYERS, D_MODEL)),
            _resident((N_SUBLAYERS, D_MODEL)),
            pl.BlockSpec((1, ext, V7X_LANES), lambda b, i: (i, 0, 0)),
            pl.BlockSpec((1, ext, V7X_LANES), lambda b, i: (i, 0, 0)),
            pl.BlockSpec(memory_space=pltpu.SMEM),
            _resident(wkvu.shape),
            _resident(wqg.shape),
            _resident(wab.shape),
            _resident(poolw.shape),
            _resident(pscale.shape),
            _resident(wpb.shape),
            _resident(wout.shape),
        ],
        out_specs=pl.BlockSpec((1, tile, D_MODEL), lambda b, i: (b, i, 0)),
        out_shape=jax.ShapeDtypeStruct(x.shape, F32),
        scratch_shapes=[
            pltpu.VMEM((ext, D_MODEL), BF16),
            pltpu.VMEM((N_HEADS, tile, HEAD_DIM), BF16),
            pltpu.VMEM((N_KV_HEADS, ext, HEAD_DIM), BF16),
            pltpu.VMEM((N_KV_HEADS, ext, HEAD_DIM), BF16),
            pltpu.VMEM((ext, POOL_WIDTH), F32),
            pltpu.VMEM((tile, ATTN_WIDTH), BF16),
            pltpu.VMEM((tile, POOL_WIDTH), BF16),
        ],
        compiler_params=pltpu.CompilerParams(dimension_semantics=("arbitrary", "arbitrary"),
                                             vmem_limit_bytes=VMEM_LIMIT),
        name=f"mixer_sublayer{sub}",
    )(x, x, x, mod, norm_pre, norm_post, cos, sin, sink, wkvu, wqg, wab, poolw, pscale, wpb, wout)


def _encoder_layer(xs, mods, norm_pre, norm_post, ffn1, mixer, ffn2):
    outs = []
    for x, mod in zip(xs, mods):
        x = _ffn_sublayer(x, mod, norm_pre, norm_post, *ffn1, sub=0)
        x = _mixer_sublayer(x, mod, norm_pre, norm_post, *mixer, sub=1)
        x = _ffn_sublayer(x, mod, norm_pre, norm_post, *ffn2, sub=2)
        outs.append(x)
    return outs


def kernel(x_prompt, x_sample, c_prompt, c_sample, ada_w, ada_b, norm_pre, norm_post, ffn1_w_gate, ffn1_w_up, ffn1_w_down, w_in, attn_sink, w_attn_branch, pool_w, pool_scale, w_pool_branch, w_out, ffn2_w_gate, ffn2_w_up, ffn2_w_down):
    xs = [x_prompt, x_sample]
    n_prompt, n_sample = c_prompt.shape[0], c_sample.shape[0]
    rows = -(-(n_prompt + n_sample) // 8) * 8
    c_all = jnp.concatenate([c_prompt, c_sample, jnp.zeros((rows - n_prompt - n_sample, D_MODEL), F32)], axis=0)
    e_q, e_v, e_u = ATTN_WIDTH, ATTN_WIDTH + 2 * KV_WIDTH, ATTN_WIDTH + 2 * KV_WIDTH + POOL_WIDTH
    for l in range(ada_w.shape[0]):
        mod = _ada_mod(c_all, ada_w[l], ada_b[l]).reshape(rows, 3 * N_SUBLAYERS, D_MODEL)
        mods = [mod[:n_prompt], mod[n_prompt:n_prompt + n_sample]]
        ffn1 = (ffn1_w_gate[l].astype(BF16), ffn1_w_up[l].astype(BF16), ffn1_w_down[l].astype(BF16))
        ffn2 = (ffn2_w_gate[l].astype(BF16), ffn2_w_up[l].astype(BF16), ffn2_w_down[l].astype(BF16))
        w_in_l = w_in[l].astype(BF16)
        mixer = (
            attn_sink[l],
            w_in_l[:, e_q:e_u],
            jnp.concatenate([w_in_l[:, :e_q], w_in_l[:, e_u:]], axis=1),
            w_attn_branch[l].astype(BF16),
            pool_w[l].astype(BF16),
            pool_scale[l].reshape(1, POOL_WIDTH),
            w_pool_branch[l].astype(BF16),
            w_out[l].astype(BF16),
        )
        xs = _encoder_layer(xs, mods, norm_pre[l], norm_post[l], ffn1, mixer, ffn2)
    return (xs[0], xs[1])
```

```python
import functools

import jax
import jax.numpy as jnp
from jax import lax
from jax.experimental import pallas as pl
from jax.experimental.pallas import tpu as pltpu

D_MODEL = 1024
N_HEADS = 8
N_KV_HEADS = 2
HEAD_DIM = 64
GROUP = N_HEADS // N_KV_HEADS
ATTN_WIDTH = N_HEADS * HEAD_DIM
KV_WIDTH = N_KV_HEADS * HEAD_DIM
WINDOW = 128
ROPE_THETA = 10000.0
POOL_WINDOWS = (2, 4, 8, 16)
POOL_GROUP_DIM = 128
POOL_WIDTH = len(POOL_WINDOWS) * POOL_GROUP_DIM
D_FF = 2816
N_SUBLAYERS = 3
FFN_RES_WEIGHT = 0.5
EPS = 1e-6
NEG_INF = -1e30
F32_MAX = 3.4028234663852886e38

V7X_LANES = 128
V7X_SUBLANES = 8
V7X_VMEM_BYTES = 64 * 1024 * 1024

HALO = WINDOW
FFN_TILE = 1024
FFN_SUBTILE = 256
MIXER_TILE = 512
ADA_COLS = 1536
VMEM_LIMIT = V7X_VMEM_BYTES - 8 * 1024 * 1024

F32 = jnp.float32
BF16 = jnp.bfloat16


def _sigmoid(x):
    return 1.0 / (1.0 + jnp.exp(-x))


def _rms_scale(x):
    return x * lax.rsqrt(jnp.mean(x * x, axis=-1, keepdims=True) + EPS)


def _resident(shape):
    return pl.BlockSpec(shape, lambda *_: (0,) * len(shape), pipeline_mode=pl.Buffered(1))


def _ada_kernel(c_ref, w_ref, b_ref, o_ref):
    c = c_ref[...]
    s = c * _sigmoid(c)
    o_ref[...] = jnp.dot(s, w_ref[...], preferred_element_type=F32,
                         precision=lax.Precision.HIGHEST) + b_ref[...]


def _ada_mod(c, ada_w, ada_b):
    rows, width = c.shape[0], ada_w.shape[1]
    return pl.pallas_call(
        _ada_kernel,
        grid=(width // ADA_COLS,),
        in_specs=[
            pl.BlockSpec((rows, D_MODEL), lambda n: (0, 0)),
            pl.BlockSpec((D_MODEL, ADA_COLS), lambda n: (0, n)),
            pl.BlockSpec((1, ADA_COLS), lambda n: (0, n)),
        ],
        out_specs=pl.BlockSpec((rows, ADA_COLS), lambda n: (0, n)),
        out_shape=jax.ShapeDtypeStruct((rows, width), F32),
        compiler_params=pltpu.CompilerParams(dimension_semantics=("arbitrary",),
                                             vmem_limit_bytes=VMEM_LIMIT),
        name="ada_mod",
    )(c, ada_w, ada_b.reshape(1, width))


def _ffn_kernel(x_ref, mod_ref, gpre_ref, gpost_ref, wg_ref, wu_ref, wd_ref, o_ref, *, sub):
    shift = mod_ref[0, 3 * sub:3 * sub + 1, :]
    scl = mod_ref[0, 3 * sub + 1:3 * sub + 2, :]
    gate = mod_ref[0, 3 * sub + 2:3 * sub + 3, :]
    gain_pre = gpre_ref[sub:sub + 1, :] * (1.0 + scl)
    gain_post = gpost_ref[sub:sub + 1, :] * (FFN_RES_WEIGHT * gate)
    for r in range(0, x_ref.shape[1], FFN_SUBTILE):
        rows = slice(r, r + FFN_SUBTILE)
        x = x_ref[0, rows, :]
        h = (_rms_scale(x) * gain_pre + shift).astype(BF16)
        g = jnp.dot(h, wg_ref[...], preferred_element_type=F32)
        u = jnp.dot(h, wu_ref[...], preferred_element_type=F32)
        a = (g * _sigmoid(g) * u).astype(BF16)
        y = jnp.dot(a, wd_ref[...], preferred_element_type=F32)
        o_ref[0, rows, :] = x + _rms_scale(y) * gain_post


def _ffn_sublayer(x, mod, norm_pre, norm_post, wg, wu, wd, sub):
    batch, seq, _ = x.shape
    tile = min(FFN_TILE, seq)
    return pl.pallas_call(
        functools.partial(_ffn_kernel, sub=sub),
        grid=(batch, seq // tile),
        in_specs=[
            pl.BlockSpec((1, tile, D_MODEL), lambda b, i: (b, i, 0)),
            pl.BlockSpec((1, 3 * N_SUBLAYERS, D_MODEL), lambda b, i: (b, 0, 0)),
            _resident((N_SUBLAYERS, D_MODEL)),
            _resident((N_SUBLAYERS, D_MODEL)),
            _resident((D_MODEL, D_FF)),
            _resident((D_MODEL, D_FF)),
            _resident((D_FF, D_MODEL)),
        ],
        out_specs=pl.BlockSpec((1, tile, D_MODEL), lambda b, i: (b, i, 0)),
        out_shape=jax.ShapeDtypeStruct(x.shape, F32),
        compiler_params=pltpu.CompilerParams(dimension_semantics=("arbitrary", "arbitrary"),
                                             vmem_limit_bytes=VMEM_LIMIT),
        name=f"ffn_sublayer{sub}",
    )(x, mod, norm_pre, norm_post, wg, wu, wd)


def _rope(x, cos, sin_signed, first_half):
    partner = jnp.where(first_half, pltpu.roll(x, V7X_LANES - HEAD_DIM // 2, 1),
                        pltpu.roll(x, HEAD_DIM // 2, 1))
    return x * cos + partner * sin_signed


def _mixer_kernel(xm_ref, xl_ref, xr_ref, mod_ref, gpre_ref, gpost_ref, cos_ref, sin_ref, sink_ref,
                  wq_ref, wkv_ref, wu_ref, wg_ref, wab_ref, poolw_ref, pscale_ref, wpb_ref, wout_ref,
                  o_ref,
                  hext_ref, qlo_ref, qhi_ref, k_ref, v_ref, u_ref, g_ref, attn_ref, mixed_ref,
                  c1_ref, c2_ref, b2_ref, c3_ref, b3_ref, a3_ref, *, sub, seq):
    tile = xm_ref.shape[1]
    ext = tile + 2 * HALO
    blk = WINDOW
    n_blk = tile // blk
    i = pl.program_id(1)
    n_tiles = pl.num_programs(1)
    shift = mod_ref[0, 3 * sub:3 * sub + 1, :]
    scl = mod_ref[0, 3 * sub + 1:3 * sub + 2, :]
    gate = mod_ref[0, 3 * sub + 2:3 * sub + 3, :]
    gain_pre = gpre_ref[sub:sub + 1, :] * (1.0 + scl)
    gain_post = gpost_ref[sub:sub + 1, :] * gate

    def pre(x):
        return (_rms_scale(x) * gain_pre + shift).astype(BF16)

    hext_ref[0:HALO, :] = pre(xl_ref[0])
    hext_ref[HALO:HALO + tile, :] = pre(xm_ref[0])
    hext_ref[HALO + tile:ext, :] = pre(xr_ref[0])

    lane = lax.broadcasted_iota(jnp.int32, (1, V7X_LANES), 1)
    first_half = (lane % HEAD_DIM) < (HEAD_DIM // 2)
    low_head = lane < HEAD_DIM
    cos = cos_ref[0]
    sin = sin_ref[0]

    kv = jnp.dot(hext_ref[...], wkv_ref[...], preferred_element_type=F32)
    k = _rope(kv[:, 0:KV_WIDTH], cos, sin, first_half)
    v = kv[:, KV_WIDTH:2 * KV_WIDTH]
    k_sw = pltpu.roll(k, HEAD_DIM, 1)
    v_sw = pltpu.roll(v, HEAD_DIM, 1)
    k_ref[0] = jnp.where(low_head, k, k_sw).astype(BF16)
    k_ref[1] = jnp.where(low_head, k_sw, k).astype(BF16)
    v_ref[0] = jnp.where(low_head, v, v_sw).astype(BF16)
    v_ref[1] = jnp.where(low_head, v_sw, v).astype(BF16)

    q = jnp.dot(hext_ref[HALO:HALO + tile, :], wq_ref[...], preferred_element_type=F32)
    cos_m = cos[HALO:HALO + tile, :]
    sin_m = sin[HALO:HALO + tile, :]
    for cb in range(ATTN_WIDTH // V7X_LANES):
        cols = slice(cb * V7X_LANES, (cb + 1) * V7X_LANES)
        qc = _rope(q[:, cols], cos_m, sin_m, first_half) * (HEAD_DIM ** -0.5)
        qlo_ref[:, cols] = jnp.where(low_head, qc, 0.0).astype(BF16)
        qhi_ref[:, cols] = jnp.where(low_head, 0.0, qc).astype(BF16)

    r = lax.broadcasted_iota(jnp.int32, (blk, 3 * blk), 0)
    jj = lax.broadcasted_iota(jnp.int32, (blk, 3 * blk), 1)
    cap_band = jnp.where((jj >= r) & (jj <= r + 2 * WINDOW), F32_MAX, NEG_INF)
    cap_first = jnp.where((jj < blk) & (i == 0), NEG_INF, cap_band)
    cap_last = jnp.where((jj >= 2 * blk) & (i == n_tiles - 1), NEG_INF, cap_band)

    g_cols = wg_ref.shape[1] // (n_blk * N_KV_HEADS)
    for b in range(n_blk):
        rows = slice(b * blk, (b + 1) * blk)
        cap = cap_band
        if b == 0:
            cap = cap_first
        if b == n_blk - 1:
            cap = jnp.minimum(cap, cap_last)
        for kh in range(N_KV_HEADS):
            q4 = jnp.concatenate(
                [ref[rows, (2 * kh + c) * V7X_LANES:(2 * kh + c + 1) * V7X_LANES]
                 for c in range(GROUP // 2) for ref in (qlo_ref, qhi_ref)], axis=0)
            kw = k_ref[kh, b * blk:b * blk + 3 * blk, :]
            vw = v_ref[kh, b * blk:b * blk + 3 * blk, :]
            s = lax.dot_general(q4, kw, (((1,), (1,)), ((), ())), preferred_element_type=F32)
            ps, inv = [], []
            for gi in range(GROUP):
                sink = sink_ref[GROUP * kh + gi]
                sg = jnp.minimum(s[gi * blk:(gi + 1) * blk, :], cap)
                m = jnp.maximum(jnp.max(sg, axis=-1, keepdims=True), sink)
                p = jnp.exp(sg - m)
                inv.append(1.0 / (jnp.sum(p, axis=-1, keepdims=True) + jnp.exp(sink - m)))
                ps.append(p.astype(BF16))
            o4 = jnp.dot(jnp.concatenate(ps, axis=0), vw, preferred_element_type=F32)
            for c in range(GROUP // 2):
                lo = o4[(2 * c) * blk:(2 * c + 1) * blk, :] * inv[2 * c]
                hi = o4[(2 * c + 1) * blk:(2 * c + 2) * blk, :] * inv[2 * c + 1]
                attn_ref[rows, (2 * kh + c) * V7X_LANES:(2 * kh + c + 1) * V7X_LANES] = (
                    jnp.where(low_head, lo, hi).astype(BF16))
            gc = slice((b * N_KV_HEADS + kh) * g_cols, (b * N_KV_HEADS + kh + 1) * g_cols)
            g_ref[:, gc] = jnp.dot(hext_ref[HALO:HALO + tile, :], wg_ref[:, gc], preferred_element_type=F32)

    u_ext = jnp.dot(hext_ref[...], wu_ref[...], preferred_element_type=F32)
    u_ref[0:HALO, :] = jnp.where(i > 0, u_ext[0:HALO, :], 0.0)
    u_ref[HALO:HALO + tile, :] = u_ext[HALO:HALO + tile, :]
    u_ref[HALO + tile:ext, :] = jnp.where(i < n_tiles - 1, u_ext[HALO + tile:ext, :], 0.0)

    a = jnp.dot(attn_ref[...], wab_ref[...], preferred_element_type=F32)

    base = HALO - V7X_SUBLANES
    pos = i * tile + lax.broadcasted_iota(jnp.int32, (tile, 1), 0)
    levels = ((), (c1_ref,), (c2_ref, b2_ref), (c3_ref, b3_ref, a3_ref))
    for gi, w in enumerate(POOL_WINDOWS):
        cols = slice(gi * POOL_GROUP_DIM, (gi + 1) * POOL_GROUP_DIM)
        if not levels[gi]:
            acc = u_ref[HALO - 1:HALO - 1 + tile, cols] + u_ref[HALO:HALO + tile, cols]
        else:
            n0 = levels[gi][0].shape[0]
            levels[gi][0][...] = u_ref[base:base + n0, cols] + u_ref[base + 1:base + 1 + n0, cols]
            for lv in range(1, len(levels[gi])):
                n = levels[gi][lv].shape[0]
                step = 2 ** lv
                levels[gi][lv][...] = levels[gi][lv - 1][0:n, :] + levels[gi][lv - 1][step:step + n, :]
            top = levels[gi][-1]
            first = V7X_SUBLANES - w // 2
            acc = top[first:first + tile, :] + top[V7X_SUBLANES:V7X_SUBLANES + tile, :]
        cnt = jnp.minimum(pos + w // 2, seq) - jnp.maximum(pos - w // 2, 0)
        pooled = acc / cnt.astype(F32) - u_ref[HALO:HALO + tile, cols]
        mixed = jnp.dot(pooled.astype(BF16), poolw_ref[gi], preferred_element_type=F32)
        mixed_ref[:, cols] = (mixed * pscale_ref[:, cols]).astype(BF16)
    p_branch = jnp.dot(mixed_ref[...], wpb_ref[...], preferred_element_type=F32)

    merged = (_sigmoid(g_ref[:, 0:D_MODEL]) * a + _sigmoid(g_ref[:, D_MODEL:2 * D_MODEL]) * p_branch).astype(BF16)
    y = jnp.dot(merged, wout_ref[...], preferred_element_type=F32)
    o_ref[0] = xm_ref[0] + _rms_scale(y) * gain_post


def _rope_tables(seq, tile):
    half = HEAD_DIM // 2
    inv_freq = ROPE_THETA ** (-jnp.arange(half, dtype=F32) / half)
    lane = jnp.arange(V7X_LANES)
    pos = (jnp.arange(seq // tile) * tile - HALO)[:, None] + jnp.arange(tile + 2 * HALO)[None, :]
    ang = pos.astype(F32)[:, :, None] * inv_freq[lane % half][None, None, :]
    sign = jnp.where((lane % HEAD_DIM) < half, -1.0, 1.0).astype(F32)
    return jnp.cos(ang), jnp.sin(ang) * sign


def _mixer_sublayer(x, mod, norm_pre, norm_post, sink, wq, wkv, wu, wg, wab, poolw, pscale, wpb, wout, sub):
    batch, seq, _ = x.shape
    tile = min(MIXER_TILE, seq)
    ext = tile + 2 * HALO
    per_tile = tile // HALO
    last_halo = seq // HALO - 1
    cos, sin = _rope_tables(seq, tile)
    return pl.pallas_call(
        functools.partial(_mixer_kernel, sub=sub, seq=seq),
        grid=(batch, seq // tile),
        in_specs=[
            pl.BlockSpec((1, tile, D_MODEL), lambda b, i: (b, i, 0)),
            pl.BlockSpec((1, HALO, D_MODEL), lambda b, i: (b, jnp.maximum(i * per_tile - 1, 0), 0)),
            pl.BlockSpec((1, HALO, D_MODEL), lambda b, i: (b, jnp.minimum((i + 1) * per_tile, last_halo), 0)),
            pl.BlockSpec((1, 3 * N_SUBLAYERS, D_MODEL), lambda b, i: (b, 0, 0)),
            _resident((N_SUBLAYERS, D_MODEL)),
            _resident((N_SUBLAYERS, D_MODEL)),
            pl.BlockSpec((1, ext, V7X_LANES), lambda b, i: (i, 0, 0)),
            pl.BlockSpec((1, ext, V7X_LANES), lambda b, i: (i, 0, 0)),
            pl.BlockSpec(memory_space=pltpu.SMEM),
            _resident(wq.shape),
            _resident(wkv.shape),
            _resident(wu.shape),
            _resident(wg.shape),
            _resident(wab.shape),
            _resident(poolw.shape),
            _resident(pscale.shape),
            _resident(wpb.shape),
            _resident(wout.shape),
        ],
        out_specs=pl.BlockSpec((1, tile, D_MODEL), lambda b, i: (b, i, 0)),
        out_shape=jax.ShapeDtypeStruct(x.shape, F32),
        scratch_shapes=[
            pltpu.VMEM((ext, D_MODEL), BF16),
            pltpu.VMEM((tile, ATTN_WIDTH), BF16),
            pltpu.VMEM((tile, ATTN_WIDTH), BF16),
            pltpu.VMEM((N_KV_HEADS, ext, V7X_LANES), BF16),
            pltpu.VMEM((N_KV_HEADS, ext, V7X_LANES), BF16),
            pltpu.VMEM((ext, POOL_WIDTH), F32),
            pltpu.VMEM((tile, 2 * D_MODEL), F32),
            pltpu.VMEM((tile, ATTN_WIDTH), BF16),
            pltpu.VMEM((tile, POOL_WIDTH), BF16),
            pltpu.VMEM((tile + 8, POOL_GROUP_DIM), F32),
            pltpu.VMEM((tile + 16, POOL_GROUP_DIM), F32),
            pltpu.VMEM((tile + 8, POOL_GROUP_DIM), F32),
            pltpu.VMEM((tile + 24, POOL_GROUP_DIM), F32),
            pltpu.VMEM((tile + 16, POOL_GROUP_DIM), F32),
            pltpu.VMEM((tile + 8, POOL_GROUP_DIM), F32),
        ],
        compiler_params=pltpu.CompilerParams(dimension_semantics=("arbitrary", "arbitrary"),
                                             vmem_limit_bytes=VMEM_LIMIT),
        name=f"mixer_sublayer{sub}",
    )(x, x, x, mod, norm_pre, norm_post, cos, sin, sink, wq, wkv, wu, wg, wab, poolw, pscale, wpb, wout)


def _encoder_layer(xs, mods, norm_pre, norm_post, ffn1, mixer, ffn2):
    outs = []
    for x, mod in zip(xs, mods):
        x = _ffn_sublayer(x, mod, norm_pre, norm_post, *ffn1, sub=0)
        x = _mixer_sublayer(x, mod, norm_pre, norm_post, *mixer, sub=1)
        x = _ffn_sublayer(x, mod, norm_pre, norm_post, *ffn2, sub=2)
        outs.append(x)
    return outs


def kernel(x_prompt, x_sample, c_prompt, c_sample, ada_w, ada_b, norm_pre, norm_post, ffn1_w_gate, ffn1_w_up, ffn1_w_down, w_in, attn_sink, w_attn_branch, pool_w, pool_scale, w_pool_branch, w_out, ffn2_w_gate, ffn2_w_up, ffn2_w_down):
    xs = [x_prompt, x_sample]
    n_prompt, n_sample = c_prompt.shape[0], c_sample.shape[0]
    rows = -(-(n_prompt + n_sample) // V7X_SUBLANES) * V7X_SUBLANES
    c_all = jnp.concatenate([c_prompt, c_sample, jnp.zeros((rows - n_prompt - n_sample, D_MODEL), F32)], axis=0)
    e_q, e_v, e_u = ATTN_WIDTH, ATTN_WIDTH + 2 * KV_WIDTH, ATTN_WIDTH + 2 * KV_WIDTH + POOL_WIDTH
    for l in range(ada_w.shape[0]):
        mod = _ada_mod(c_all, ada_w[l], ada_b[l]).reshape(rows, 3 * N_SUBLAYERS, D_MODEL)
        mods = [mod[:n_prompt], mod[n_prompt:n_prompt + n_sample]]
        ffn1 = (ffn1_w_gate[l].astype(BF16), ffn1_w_up[l].astype(BF16), ffn1_w_down[l].astype(BF16))
        ffn2 = (ffn2_w_gate[l].astype(BF16), ffn2_w_up[l].astype(BF16), ffn2_w_down[l].astype(BF16))
        w_in_l = w_in[l].astype(BF16)
        mixer = (
            attn_sink[l],
            w_in_l[:, :e_q],
            w_in_l[:, e_q:e_v],
            w_in_l[:, e_v:e_u],
            w_in_l[:, e_u:],
            w_attn_branch[l].astype(BF16),
            pool_w[l].astype(BF16),
            pool_scale[l].reshape(1, POOL_WIDTH),
            w_pool_branch[l].astype(BF16),
            w_out[l].astype(BF16),
        )
        xs = _encoder_layer(xs, mods, norm_pre[l], norm_post[l], ffn1, mixer, ffn2)
    return (xs[0], xs[1])
```

```python
import functools

import jax
import jax.numpy as jnp
from jax import lax
from jax.experimental import pallas as pl
from jax.experimental.pallas import tpu as pltpu

D_MODEL = 1024
N_HEADS = 8
N_KV_HEADS = 2
HEAD_DIM = 64
GROUP = N_HEADS // N_KV_HEADS
ATTN_WIDTH = N_HEADS * HEAD_DIM
KV_WIDTH = N_KV_HEADS * HEAD_DIM
WINDOW = 128
ROPE_THETA = 10000.0
POOL_WINDOWS = (2, 4, 8, 16)
POOL_GROUP_DIM = 128
POOL_WIDTH = len(POOL_WINDOWS) * POOL_GROUP_DIM
D_FF = 2816
N_SUBLAYERS = 3
FFN_RES_WEIGHT = 0.5
EPS = 1e-6
NEG_INF = -1e30
F32_MAX = 3.4028234663852886e38

V7X_LANES = 128
V7X_SUBLANES = 8
V7X_VMEM_BYTES = 64 * 1024 * 1024

HALO = WINDOW
FFN_TILE = 1024
FFN_SUBTILE = 256
MIXER_TILE = 512
ADA_COLS = 1536
VMEM_LIMIT = V7X_VMEM_BYTES - 8 * 1024 * 1024

F32 = jnp.float32
BF16 = jnp.bfloat16


def _sigmoid(x):
    return 1.0 / (1.0 + jnp.exp(-x))


def _rms_scale(x):
    return x * lax.rsqrt(jnp.mean(x * x, axis=-1, keepdims=True) + EPS)


def _resident(shape):
    return pl.BlockSpec(shape, lambda *_: (0,) * len(shape), pipeline_mode=pl.Buffered(1))


def _ada_kernel(c_ref, w_ref, b_ref, o_ref):
    c = c_ref[...]
    s = c * _sigmoid(c)
    o_ref[...] = jnp.dot(s, w_ref[...], preferred_element_type=F32,
                         precision=lax.Precision.HIGHEST) + b_ref[...]


def _ada_mod(c, ada_w, ada_b):
    rows, width = c.shape[0], ada_w.shape[1]
    return pl.pallas_call(
        _ada_kernel,
        grid=(width // ADA_COLS,),
        in_specs=[
            pl.BlockSpec((rows, D_MODEL), lambda n: (0, 0)),
            pl.BlockSpec((D_MODEL, ADA_COLS), lambda n: (0, n)),
            pl.BlockSpec((1, ADA_COLS), lambda n: (0, n)),
        ],
        out_specs=pl.BlockSpec((rows, ADA_COLS), lambda n: (0, n)),
        out_shape=jax.ShapeDtypeStruct((rows, width), F32),
        compiler_params=pltpu.CompilerParams(dimension_semantics=("arbitrary",),
                                             vmem_limit_bytes=VMEM_LIMIT),
        name="ada_mod",
    )(c, ada_w, ada_b.reshape(1, width))


def _ffn_kernel(x_ref, mod_ref, gpre_ref, gpost_ref, wg_ref, wu_ref, wd_ref, o_ref, *, sub):
    shift = mod_ref[0, 3 * sub:3 * sub + 1, :]
    scl = mod_ref[0, 3 * sub + 1:3 * sub + 2, :]
    gate = mod_ref[0, 3 * sub + 2:3 * sub + 3, :]
    gain_pre = gpre_ref[sub:sub + 1, :] * (1.0 + scl)
    gain_post = gpost_ref[sub:sub + 1, :] * (FFN_RES_WEIGHT * gate)
    n_sub = x_ref.shape[1] // FFN_SUBTILE

    def rows(k):
        return slice(k * FFN_SUBTILE, (k + 1) * FFN_SUBTILE)

    def pre(k):
        return (_rms_scale(x_ref[0, rows(k), :]) * gain_pre + shift).astype(BF16)

    def post(k, y):
        o_ref[0, rows(k), :] = x_ref[0, rows(k), :] + _rms_scale(y) * gain_post

    h = pre(0)
    y_prev = None
    for k in range(n_sub):
        g = jnp.dot(h, wg_ref[...], preferred_element_type=F32)
        u = jnp.dot(h, wu_ref[...], preferred_element_type=F32)
        if k + 1 < n_sub:
            h = pre(k + 1)
        if y_prev is not None:
            post(k - 1, y_prev)
        a = (g * _sigmoid(g) * u).astype(BF16)
        y_prev = jnp.dot(a, wd_ref[...], preferred_element_type=F32)
    post(n_sub - 1, y_prev)


def _ffn_sublayer(x, mod, norm_pre, norm_post, wg, wu, wd, sub):
    batch, seq, _ = x.shape
    tile = min(FFN_TILE, seq)
    return pl.pallas_call(
        functools.partial(_ffn_kernel, sub=sub),
        grid=(batch, seq // tile),
        in_specs=[
            pl.BlockSpec((1, tile, D_MODEL), lambda b, i: (b, i, 0)),
            pl.BlockSpec((1, 3 * N_SUBLAYERS, D_MODEL), lambda b, i: (b, 0, 0)),
            _resident((N_SUBLAYERS, D_MODEL)),
            _resident((N_SUBLAYERS, D_MODEL)),
            _resident((D_MODEL, D_FF)),
            _resident((D_MODEL, D_FF)),
            _resident((D_FF, D_MODEL)),
        ],
        out_specs=pl.BlockSpec((1, tile, D_MODEL), lambda b, i: (b, i, 0)),
        out_shape=jax.ShapeDtypeStruct(x.shape, F32),
        compiler_params=pltpu.CompilerParams(dimension_semantics=("arbitrary", "arbitrary"),
                                             vmem_limit_bytes=VMEM_LIMIT),
        name=f"ffn_sublayer{sub}",
    )(x, mod, norm_pre, norm_post, wg, wu, wd)


def _rope(x, cos, sin_signed, first_half):
    partner = jnp.where(first_half, pltpu.roll(x, V7X_LANES - HEAD_DIM // 2, 1),
                        pltpu.roll(x, HEAD_DIM // 2, 1))
    return x * cos + partner * sin_signed


def _mixer_kernel(xm_ref, xl_ref, xr_ref, mod_ref, gpre_ref, gpost_ref, cos_ref, sin_ref, sink_ref,
                  wq_ref, wkv_ref, wu_ref, wg_ref, wab_ref, poolw_ref, pscale_ref, wpb_ref, wout_ref,
                  o_ref,
                  hext_ref, qlo_ref, qhi_ref, k_ref, v_ref, u_ref, g_ref, attn_ref, mixed_ref,
                  c1_ref, c2_ref, b2_ref, c3_ref, b3_ref, a3_ref, *, sub, seq):
    tile = xm_ref.shape[1]
    ext = tile + 2 * HALO
    blk = WINDOW
    n_blk = tile // blk
    i = pl.program_id(1)
    n_tiles = pl.num_programs(1)
    shift = mod_ref[0, 3 * sub:3 * sub + 1, :]
    scl = mod_ref[0, 3 * sub + 1:3 * sub + 2, :]
    gate = mod_ref[0, 3 * sub + 2:3 * sub + 3, :]
    gain_pre = gpre_ref[sub:sub + 1, :] * (1.0 + scl)
    gain_post = gpost_ref[sub:sub + 1, :] * gate

    def pre(x):
        return (_rms_scale(x) * gain_pre + shift).astype(BF16)

    hext_ref[0:HALO, :] = pre(xl_ref[0])
    hext_ref[HALO:HALO + tile, :] = pre(xm_ref[0])
    hext_ref[HALO + tile:ext, :] = pre(xr_ref[0])

    lane = lax.broadcasted_iota(jnp.int32, (1, V7X_LANES), 1)
    first_half = (lane % HEAD_DIM) < (HEAD_DIM // 2)
    low_head = lane < HEAD_DIM
    cos = cos_ref[0]
    sin = sin_ref[0]

    kv = jnp.dot(hext_ref[...], wkv_ref[...], preferred_element_type=F32)
    k = _rope(kv[:, 0:KV_WIDTH], cos, sin, first_half)
    v = kv[:, KV_WIDTH:2 * KV_WIDTH]
    k_sw = pltpu.roll(k, HEAD_DIM, 1)
    v_sw = pltpu.roll(v, HEAD_DIM, 1)
    k_ref[0] = jnp.where(low_head, k, k_sw).astype(BF16)
    k_ref[1] = jnp.where(low_head, k_sw, k).astype(BF16)
    v_ref[0] = jnp.where(low_head, v, v_sw).astype(BF16)
    v_ref[1] = jnp.where(low_head, v_sw, v).astype(BF16)

    q = jnp.dot(hext_ref[HALO:HALO + tile, :], wq_ref[...], preferred_element_type=F32)
    cos_m = cos[HALO:HALO + tile, :]
    sin_m = sin[HALO:HALO + tile, :]
    for cb in range(ATTN_WIDTH // V7X_LANES):
        cols = slice(cb * V7X_LANES, (cb + 1) * V7X_LANES)
        qc = _rope(q[:, cols], cos_m, sin_m, first_half) * (HEAD_DIM ** -0.5)
        qlo_ref[:, cols] = jnp.where(low_head, qc, 0.0).astype(BF16)
        qhi_ref[:, cols] = jnp.where(low_head, 0.0, qc).astype(BF16)

    u_ext = jnp.dot(hext_ref[...], wu_ref[...], preferred_element_type=F32)
    u_ref[0:HALO, :] = jnp.where(i > 0, u_ext[0:HALO, :], 0.0)
    u_ref[HALO:HALO + tile, :] = u_ext[HALO:HALO + tile, :]
    u_ref[HALO + tile:ext, :] = jnp.where(i < n_tiles - 1, u_ext[HALO + tile:ext, :], 0.0)

    base = HALO - V7X_SUBLANES
    pos = i * tile + lax.broadcasted_iota(jnp.int32, (tile, 1), 0)
    levels = ((), (c1_ref,), (c2_ref, b2_ref), (c3_ref, b3_ref, a3_ref))

    def pool_group(gi):
        w = POOL_WINDOWS[gi]
        cols = slice(gi * POOL_GROUP_DIM, (gi + 1) * POOL_GROUP_DIM)
        if not levels[gi]:
            acc = u_ref[HALO - 1:HALO - 1 + tile, cols] + u_ref[HALO:HALO + tile, cols]
        else:
            n0 = levels[gi][0].shape[0]
            levels[gi][0][...] = u_ref[base:base + n0, cols] + u_ref[base + 1:base + 1 + n0, cols]
            for lv in range(1, len(levels[gi])):
                n = levels[gi][lv].shape[0]
                step = 2 ** lv
                levels[gi][lv][...] = levels[gi][lv - 1][0:n, :] + levels[gi][lv - 1][step:step + n, :]
            top = levels[gi][-1]
            first = V7X_SUBLANES - w // 2
            acc = top[first:first + tile, :] + top[V7X_SUBLANES:V7X_SUBLANES + tile, :]
        cnt = jnp.minimum(pos + w // 2, seq) - jnp.maximum(pos - w // 2, 0)
        pooled = acc / cnt.astype(F32) - u_ref[HALO:HALO + tile, cols]
        mixed = jnp.dot(pooled.astype(BF16), poolw_ref[gi], preferred_element_type=F32)
        mixed_ref[:, cols] = (mixed * pscale_ref[:, cols]).astype(BF16)

    r = lax.broadcasted_iota(jnp.int32, (blk, 3 * blk), 0)
    jj = lax.broadcasted_iota(jnp.int32, (blk, 3 * blk), 1)
    cap_band = jnp.where((jj >= r) & (jj <= r + 2 * WINDOW), F32_MAX, NEG_INF)
    cap_first = jnp.where((jj < blk) & (i == 0), NEG_INF, cap_band)
    cap_last = jnp.where((jj >= 2 * blk) & (i == n_tiles - 1), NEG_INF, cap_band)

    g_cols = wg_ref.shape[1] // (n_blk * N_KV_HEADS)
    instances = [(b, kh) for b in range(n_blk) for kh in range(N_KV_HEADS)]

    def scores(b, kh):
        rows = slice(b * blk, (b + 1) * blk)
        q4 = jnp.concatenate(
            [ref[rows, (2 * kh + c) * V7X_LANES:(2 * kh + c + 1) * V7X_LANES]
             for c in range(GROUP // 2) for ref in (qlo_ref, qhi_ref)], axis=0)
        kw = k_ref[kh, b * blk:b * blk + 3 * blk, :]
        return lax.dot_general(q4, kw, (((1,), (1,)), ((), ())), preferred_element_type=F32)

    s_next = scores(*instances[0])
    for n, (b, kh) in enumerate(instances):
        rows = slice(b * blk, (b + 1) * blk)
        cap = cap_band
        if b == 0:
            cap = cap_first
        if b == n_blk - 1:
            cap = jnp.minimum(cap, cap_last)
        s = s_next
        if n + 1 < len(instances):
            s_next = scores(*instances[n + 1])
        vw = v_ref[kh, b * blk:b * blk + 3 * blk, :]
        ps, inv = [], []
        for gi in range(GROUP):
            sink = sink_ref[GROUP * kh + gi]
            sg = jnp.minimum(s[gi * blk:(gi + 1) * blk, :], cap)
            m = jnp.maximum(jnp.max(sg, axis=-1, keepdims=True), sink)
            p = jnp.exp(sg - m)
            inv.append(1.0 / (jnp.sum(p, axis=-1, keepdims=True) + jnp.exp(sink - m)))
            ps.append(p.astype(BF16))
        o4 = jnp.dot(jnp.concatenate(ps, axis=0), vw, preferred_element_type=F32)
        for c in range(GROUP // 2):
            lo = o4[(2 * c) * blk:(2 * c + 1) * blk, :] * inv[2 * c]
            hi = o4[(2 * c + 1) * blk:(2 * c + 2) * blk, :] * inv[2 * c + 1]
            attn_ref[rows, (2 * kh + c) * V7X_LANES:(2 * kh + c + 1) * V7X_LANES] = (
                jnp.where(low_head, lo, hi).astype(BF16))
        gc = slice(n * g_cols, (n + 1) * g_cols)
        g_ref[:, gc] = jnp.dot(hext_ref[HALO:HALO + tile, :], wg_ref[:, gc], preferred_element_type=F32)
        if n % 2 == 1 and n // 2 < len(POOL_WINDOWS):
            pool_group(n // 2)
    for gi in range(len(instances) // 2, len(POOL_WINDOWS)):
        pool_group(gi)

    a = jnp.dot(attn_ref[...], wab_ref[...], preferred_element_type=F32)

    p_branch = jnp.dot(mixed_ref[...], wpb_ref[...], preferred_element_type=F32)

    merged = (_sigmoid(g_ref[:, 0:D_MODEL]) * a + _sigmoid(g_ref[:, D_MODEL:2 * D_MODEL]) * p_branch).astype(BF16)
    y = jnp.dot(merged, wout_ref[...], preferred_element_type=F32)
    o_ref[0] = xm_ref[0] + _rms_scale(y) * gain_post


def _rope_tables(seq, tile):
    half = HEAD_DIM // 2
    inv_freq = ROPE_THETA ** (-jnp.arange(half, dtype=F32) / half)
    pos = jnp.arange(-HALO, seq + HALO)
    ang = pos.astype(F32)[:, None] * inv_freq[None, :]
    sign = jnp.where((jnp.arange(V7X_LANES) % HEAD_DIM) < half, -1.0, 1.0).astype(F32)
    cos = jnp.tile(jnp.cos(ang), (1, V7X_LANES // half))
    sin = jnp.tile(jnp.sin(ang), (1, V7X_LANES // half)) * sign
    ext = tile + 2 * HALO
    windows = lambda t: jnp.stack([t[i * tile:i * tile + ext] for i in range(seq // tile)])
    return windows(cos), windows(sin)


def _mixer_sublayer(x, mod, norm_pre, norm_post, sink, wq, wkv, wu, wg, wab, poolw, pscale, wpb, wout, sub):
    batch, seq, _ = x.shape
    tile = min(MIXER_TILE, seq)
    ext = tile + 2 * HALO
    per_tile = tile // HALO
    last_halo = seq // HALO - 1
    cos, sin = _rope_tables(seq, tile)
    return pl.pallas_call(
        functools.partial(_mixer_kernel, sub=sub, seq=seq),
        grid=(batch, seq // tile),
        in_specs=[
            pl.BlockSpec((1, tile, D_MODEL), lambda b, i: (b, i, 0)),
            pl.BlockSpec((1, HALO, D_MODEL), lambda b, i: (b, jnp.maximum(i * per_tile - 1, 0), 0)),
            pl.BlockSpec((1, HALO, D_MODEL), lambda b, i: (b, jnp.minimum((i + 1) * per_tile, last_halo), 0)),
            pl.BlockSpec((1, 3 * N_SUBLAYERS, D_MODEL), lambda b, i: (b, 0, 0)),
            _resident((N_SUBLAYERS, D_MODEL)),
            _resident((N_SUBLAYERS, D_MODEL)),
            pl.BlockSpec((1, ext, V7X_LANES), lambda b, i: (i, 0, 0)),
            pl.BlockSpec((1, ext, V7X_LANES), lambda b, i: (i, 0, 0)),
            pl.BlockSpec(memory_space=pltpu.SMEM),
            _resident(wq.shape),
            _resident(wkv.shape),
            _resident(wu.shape),
            _resident(wg.shape),
            _resident(wab.shape),
            _resident(poolw.shape),
            _resident(pscale.shape),
            _resident(wpb.shape),
            _resident(wout.shape),
        ],
        out_specs=pl.BlockSpec((1, tile, D_MODEL), lambda b, i: (b, i, 0)),
        out_shape=jax.ShapeDtypeStruct(x.shape, F32),
        scratch_shapes=[
            pltpu.VMEM((ext, D_MODEL), BF16),
            pltpu.VMEM((tile, ATTN_WIDTH), BF16),
            pltpu.VMEM((tile, ATTN_WIDTH), BF16),
            pltpu.VMEM((N_KV_HEADS, ext, V7X_LANES), BF16),
            pltpu.VMEM((N_KV_HEADS, ext, V7X_LANES), BF16),
            pltpu.VMEM((ext, POOL_WIDTH), F32),
            pltpu.VMEM((tile, 2 * D_MODEL), F32),
            pltpu.VMEM((tile, ATTN_WIDTH), BF16),
            pltpu.VMEM((tile, POOL_WIDTH), BF16),
            pltpu.VMEM((tile + 8, POOL_GROUP_DIM), F32),
            pltpu.VMEM((tile + 16, POOL_GROUP_DIM), F32),
            pltpu.VMEM((tile + 8, POOL_GROUP_DIM), F32),
            pltpu.VMEM((tile + 24, POOL_GROUP_DIM), F32),
            pltpu.VMEM((tile + 16, POOL_GROUP_DIM), F32),
            pltpu.VMEM((tile + 8, POOL_GROUP_DIM), F32),
        ],
        compiler_params=pltpu.CompilerParams(dimension_semantics=("arbitrary", "arbitrary"),
                                             vmem_limit_bytes=VMEM_LIMIT),
        name=f"mixer_sublayer{sub}",
    )(x, x, x, mod, norm_pre, norm_post, cos, sin, sink, wq, wkv, wu, wg, wab, poolw, pscale, wpb, wout)


def _encoder_layer(xs, mods, norm_pre, norm_post, ffn1, mixer, ffn2):
    outs = []
    for x, mod in zip(xs, mods):
        x = _ffn_sublayer(x, mod, norm_pre, norm_post, *ffn1, sub=0)
        x = _mixer_sublayer(x, mod, norm_pre, norm_post, *mixer, sub=1)
        x = _ffn_sublayer(x, mod, norm_pre, norm_post, *ffn2, sub=2)
        outs.append(x)
    return outs


def kernel(x_prompt, x_sample, c_prompt, c_sample, ada_w, ada_b, norm_pre, norm_post, ffn1_w_gate, ffn1_w_up, ffn1_w_down, w_in, attn_sink, w_attn_branch, pool_w, pool_scale, w_pool_branch, w_out, ffn2_w_gate, ffn2_w_up, ffn2_w_down):
    xs = [x_prompt, x_sample]
    n_prompt, n_sample = c_prompt.shape[0], c_sample.shape[0]
    rows = -(-(n_prompt + n_sample) // V7X_SUBLANES) * V7X_SUBLANES
    c_all = jnp.concatenate([c_prompt, c_sample, jnp.zeros((rows - n_prompt - n_sample, D_MODEL), F32)], axis=0)
    e_q, e_v, e_u = ATTN_WIDTH, ATTN_WIDTH + 2 * KV_WIDTH, ATTN_WIDTH + 2 * KV_WIDTH + POOL_WIDTH
    for l in range(ada_w.shape[0]):
        mod = _ada_mod(c_all, ada_w[l], ada_b[l]).reshape(rows, 3 * N_SUBLAYERS, D_MODEL)
        mods = [mod[:n_prompt], mod[n_prompt:n_prompt + n_sample]]
        ffn1 = (ffn1_w_gate[l].astype(BF16), ffn1_w_up[l].astype(BF16), ffn1_w_down[l].astype(BF16))
        ffn2 = (ffn2_w_gate[l].astype(BF16), ffn2_w_up[l].astype(BF16), ffn2_w_down[l].astype(BF16))
        w_in_l = w_in[l].astype(BF16)
        mixer = (
            attn_sink[l],
            w_in_l[:, :e_q],
            w_in_l[:, e_q:e_v],
            w_in_l[:, e_v:e_u],
            w_in_l[:, e_u:],
            w_attn_branch[l].astype(BF16),
            pool_w[l].astype(BF16),
            pool_scale[l].reshape(1, POOL_WIDTH),
            w_pool_branch[l].astype(BF16),
            w_out[l].astype(BF16),
        )
        xs = _encoder_layer(xs, mods, norm_pre[l], norm_post[l], ffn1, mixer, ffn2)
    return (xs[0], xs[1])
```

```python
import functools

import jax
import jax.numpy as jnp
from jax import lax
from jax.experimental import pallas as pl
from jax.experimental.pallas import tpu as pltpu

D_MODEL = 1024
N_HEADS = 8
N_KV_HEADS = 2
HEAD_DIM = 64
GROUP = N_HEADS // N_KV_HEADS
ATTN_WIDTH = N_HEADS * HEAD_DIM
KV_WIDTH = N_KV_HEADS * HEAD_DIM
WINDOW = 128
ROPE_THETA = 10000.0
POOL_WINDOWS = (2, 4, 8, 16)
POOL_GROUP_DIM = 128
POOL_WIDTH = len(POOL_WINDOWS) * POOL_GROUP_DIM
D_FF = 2816
N_SUBLAYERS = 3
FFN_RES_WEIGHT = 0.5
EPS = 1e-6
NEG_INF = -1e30
F32_MAX = 3.4028234663852886e38

V7X_LANES = 128
V7X_SUBLANES = 8
V7X_VMEM_BYTES = 64 * 1024 * 1024

HALO = WINDOW
FFN_TILE = 1024
FFN_SUBTILE = 256
MIXER_TILE = 512
ADA_COLS = 1536
VMEM_LIMIT = V7X_VMEM_BYTES - 8 * 1024 * 1024

F32 = jnp.float32
BF16 = jnp.bfloat16


def _sigmoid(x):
    return 1.0 / (1.0 + jnp.exp(-x))


def _rms_scale(x):
    return x * lax.rsqrt(jnp.mean(x * x, axis=-1, keepdims=True) + EPS)


def _resident(shape):
    return pl.BlockSpec(shape, lambda *_: (0,) * len(shape), pipeline_mode=pl.Buffered(1))


def _ada_kernel(c_ref, w_ref, b_ref, o_ref):
    c = c_ref[...]
    s = c * _sigmoid(c)
    o_ref[...] = jnp.dot(s, w_ref[...], preferred_element_type=F32,
                         precision=lax.Precision.HIGHEST) + b_ref[...]


def _ada_mod(c, ada_w, ada_b):
    rows, width = c.shape[0], ada_w.shape[1]
    return pl.pallas_call(
        _ada_kernel,
        grid=(width // ADA_COLS,),
        in_specs=[
            pl.BlockSpec((rows, D_MODEL), lambda n: (0, 0)),
            pl.BlockSpec((D_MODEL, ADA_COLS), lambda n: (0, n)),
            pl.BlockSpec((1, ADA_COLS), lambda n: (0, n)),
        ],
        out_specs=pl.BlockSpec((rows, ADA_COLS), lambda n: (0, n)),
        out_shape=jax.ShapeDtypeStruct((rows, width), F32),
        compiler_params=pltpu.CompilerParams(dimension_semantics=("arbitrary",),
                                             vmem_limit_bytes=VMEM_LIMIT),
        name="ada_mod",
    )(c, ada_w, ada_b.reshape(1, width))


def _ffn_kernel(x_ref, mod_ref, gpre_ref, gpost_ref, wg_ref, wu_ref, wd_ref, o_ref, *, sub):
    shift = mod_ref[0, 3 * sub:3 * sub + 1, :]
    scl = mod_ref[0, 3 * sub + 1:3 * sub + 2, :]
    gate = mod_ref[0, 3 * sub + 2:3 * sub + 3, :]
    gain_pre = gpre_ref[sub:sub + 1, :] * (1.0 + scl)
    gain_post = gpost_ref[sub:sub + 1, :] * (FFN_RES_WEIGHT * gate)
    n_sub = x_ref.shape[1] // FFN_SUBTILE

    def rows(k):
        return slice(k * FFN_SUBTILE, (k + 1) * FFN_SUBTILE)

    def pre(k):
        return (_rms_scale(x_ref[0, rows(k), :]) * gain_pre + shift).astype(BF16)

    def post(k, y):
        o_ref[0, rows(k), :] = x_ref[0, rows(k), :] + _rms_scale(y) * gain_post

    h = pre(0)
    y_prev = None
    for k in range(n_sub):
        g = jnp.dot(h, wg_ref[...], preferred_element_type=F32)
        u = jnp.dot(h, wu_ref[...], preferred_element_type=F32)
        if k + 1 < n_sub:
            h = pre(k + 1)
        if y_prev is not None:
            post(k - 1, y_prev)
        a = (g * _sigmoid(g) * u).astype(BF16)
        y_prev = jnp.dot(a, wd_ref[...], preferred_element_type=F32)
    post(n_sub - 1, y_prev)


def _ffn_sublayer(x, mod, norm_pre, norm_post, wg, wu, wd, sub):
    batch, seq, _ = x.shape
    tile = min(FFN_TILE, seq)
    return pl.pallas_call(
        functools.partial(_ffn_kernel, sub=sub),
        grid=(batch, seq // tile),
        in_specs=[
            pl.BlockSpec((1, tile, D_MODEL), lambda b, i: (b, i, 0)),
            pl.BlockSpec((1, 3 * N_SUBLAYERS, D_MODEL), lambda b, i: (b, 0, 0)),
            _resident((N_SUBLAYERS, D_MODEL)),
            _resident((N_SUBLAYERS, D_MODEL)),
            _resident((D_MODEL, D_FF)),
            _resident((D_MODEL, D_FF)),
            _resident((D_FF, D_MODEL)),
        ],
        out_specs=pl.BlockSpec((1, tile, D_MODEL), lambda b, i: (b, i, 0)),
        out_shape=jax.ShapeDtypeStruct(x.shape, F32),
        compiler_params=pltpu.CompilerParams(dimension_semantics=("arbitrary", "arbitrary"),
                                             vmem_limit_bytes=VMEM_LIMIT),
        name=f"ffn_sublayer{sub}",
    )(x, mod, norm_pre, norm_post, wg, wu, wd)


def _rope(x, cos, sin_signed, first_half):
    partner = jnp.where(first_half, pltpu.roll(x, V7X_LANES - HEAD_DIM // 2, 1),
                        pltpu.roll(x, HEAD_DIM // 2, 1))
    return x * cos + partner * sin_signed


def _mixer_kernel(xm_ref, xl_ref, xr_ref, mod_ref, gpre_ref, gpost_ref, cos_ref, sin_ref, sink_ref,
                  wq_ref, wkv_ref, wu_ref, wg_ref, wab_ref, poolw_ref, pscale_ref, wpb_ref, wout_ref,
                  o_ref,
                  hext_ref, qlo_ref, qhi_ref, k_ref, v_ref, u_ref, g_ref, attn_ref, mixed_ref,
                  c1_ref, c2_ref, b2_ref, c3_ref, b3_ref, a3_ref, *, sub, seq):
    tile = xm_ref.shape[1]
    ext = tile + 2 * HALO
    blk = WINDOW
    n_blk = tile // blk
    i = pl.program_id(1)
    n_tiles = pl.num_programs(1)
    shift = mod_ref[0, 3 * sub:3 * sub + 1, :]
    scl = mod_ref[0, 3 * sub + 1:3 * sub + 2, :]
    gate = mod_ref[0, 3 * sub + 2:3 * sub + 3, :]
    gain_pre = gpre_ref[sub:sub + 1, :] * (1.0 + scl)
    gain_post = gpost_ref[sub:sub + 1, :] * gate

    def pre(x):
        return (_rms_scale(x) * gain_pre + shift).astype(BF16)

    hext_ref[0:HALO, :] = pre(xl_ref[0])
    hext_ref[HALO:HALO + tile, :] = pre(xm_ref[0])
    hext_ref[HALO + tile:ext, :] = pre(xr_ref[0])

    lane = lax.broadcasted_iota(jnp.int32, (1, V7X_LANES), 1)
    first_half = (lane % HEAD_DIM) < (HEAD_DIM // 2)
    low_head = lane < HEAD_DIM
    cos = cos_ref[0]
    sin = sin_ref[0]

    kv = jnp.dot(hext_ref[...], wkv_ref[...], preferred_element_type=F32)
    k = _rope(kv[:, 0:KV_WIDTH], cos, sin, first_half)
    v = kv[:, KV_WIDTH:2 * KV_WIDTH]
    k_sw = pltpu.roll(k, HEAD_DIM, 1)
    v_sw = pltpu.roll(v, HEAD_DIM, 1)
    k_ref[0] = jnp.where(low_head, k, k_sw).astype(BF16)
    k_ref[1] = jnp.where(low_head, k_sw, k).astype(BF16)
    v_ref[0] = jnp.where(low_head, v, 1.0).astype(BF16)
    v_ref[1] = jnp.where(low_head, v_sw, 1.0).astype(BF16)

    q = jnp.dot(hext_ref[HALO:HALO + tile, :], wq_ref[...], preferred_element_type=F32)
    cos_m = cos[HALO:HALO + tile, :]
    sin_m = sin[HALO:HALO + tile, :]
    for cb in range(ATTN_WIDTH // V7X_LANES):
        cols = slice(cb * V7X_LANES, (cb + 1) * V7X_LANES)
        qc = _rope(q[:, cols], cos_m, sin_m, first_half) * (HEAD_DIM ** -0.5)
        qlo_ref[:, cols] = jnp.where(low_head, qc, 0.0).astype(BF16)
        qhi_ref[:, cols] = jnp.where(low_head, 0.0, qc).astype(BF16)

    u_ext = jnp.dot(hext_ref[...], wu_ref[...], preferred_element_type=F32)
    u_ref[0:HALO, :] = jnp.where(i > 0, u_ext[0:HALO, :], 0.0)
    u_ref[HALO:HALO + tile, :] = u_ext[HALO:HALO + tile, :]
    u_ref[HALO + tile:ext, :] = jnp.where(i < n_tiles - 1, u_ext[HALO + tile:ext, :], 0.0)

    base = HALO - V7X_SUBLANES
    pos = i * tile + lax.broadcasted_iota(jnp.int32, (tile, 1), 0)
    levels = ((), (c1_ref,), (c2_ref, b2_ref), (c3_ref, b3_ref, a3_ref))

    def pool_group(gi):
        w = POOL_WINDOWS[gi]
        cols = slice(gi * POOL_GROUP_DIM, (gi + 1) * POOL_GROUP_DIM)
        if not levels[gi]:
            acc = u_ref[HALO - 1:HALO - 1 + tile, cols] + u_ref[HALO:HALO + tile, cols]
        else:
            n0 = levels[gi][0].shape[0]
            levels[gi][0][...] = u_ref[base:base + n0, cols] + u_ref[base + 1:base + 1 + n0, cols]
            for lv in range(1, len(levels[gi])):
                n = levels[gi][lv].shape[0]
                step = 2 ** lv
                levels[gi][lv][...] = levels[gi][lv - 1][0:n, :] + levels[gi][lv - 1][step:step + n, :]
            top = levels[gi][-1]
            first = V7X_SUBLANES - w // 2
            acc = top[first:first + tile, :] + top[V7X_SUBLANES:V7X_SUBLANES + tile, :]
        cnt = jnp.minimum(pos + w // 2, seq) - jnp.maximum(pos - w // 2, 0)
        pooled = acc / cnt.astype(F32) - u_ref[HALO:HALO + tile, cols]
        mixed = jnp.dot(pooled.astype(BF16), poolw_ref[gi], preferred_element_type=F32)
        mixed_ref[:, cols] = (mixed * pscale_ref[:, cols]).astype(BF16)

    r = lax.broadcasted_iota(jnp.int32, (blk, 3 * blk), 0)
    jj = lax.broadcasted_iota(jnp.int32, (blk, 3 * blk), 1)
    cap_band = jnp.where((jj >= r) & (jj <= r + 2 * WINDOW), F32_MAX, NEG_INF)
    cap_first = jnp.where((jj < blk) & (i == 0), NEG_INF, cap_band)
    cap_last = jnp.where((jj >= 2 * blk) & (i == n_tiles - 1), NEG_INF, cap_band)

    g_cols = wg_ref.shape[1] // (n_blk * N_KV_HEADS)
    instances = [(b, kh) for b in range(n_blk) for kh in range(N_KV_HEADS)]

    def scores(b, kh):
        rows = slice(b * blk, (b + 1) * blk)
        q4 = jnp.concatenate(
            [ref[rows, (2 * kh + c) * V7X_LANES:(2 * kh + c + 1) * V7X_LANES]
             for c in range(GROUP // 2) for ref in (qlo_ref, qhi_ref)], axis=0)
        kw = k_ref[kh, b * blk:b * blk + 3 * blk, :]
        return lax.dot_general(q4, kw, (((1,), (1,)), ((), ())), preferred_element_type=F32)

    s_next = scores(*instances[0])
    for n, (b, kh) in enumerate(instances):
        rows = slice(b * blk, (b + 1) * blk)
        cap = cap_band
        if b == 0:
            cap = cap_first
        if b == n_blk - 1:
            cap = jnp.minimum(cap, cap_last)
        s = s_next
        if n + 1 < len(instances):
            s_next = scores(*instances[n + 1])
        vw = v_ref[kh, b * blk:b * blk + 3 * blk, :]
        ps, sink_terms = [], []
        for gi in range(GROUP):
            sink = sink_ref[GROUP * kh + gi]
            sh = s[gi * blk:(gi + 1) * blk, :]
            sg = jnp.concatenate([jnp.minimum(sh[:, :blk], cap[:, :blk]), sh[:, blk:2 * blk],
                                  jnp.minimum(sh[:, 2 * blk:], cap[:, 2 * blk:])], axis=1)
            m = jnp.maximum(jnp.max(sg, axis=-1, keepdims=True), sink)
            ps.append(jnp.exp(sg - m).astype(BF16))
            sink_terms.append(jnp.exp(sink - m))
        o4 = jnp.dot(jnp.concatenate(ps, axis=0), vw, preferred_element_type=F32)
        for c in range(GROUP // 2):
            lo = o4[(2 * c) * blk:(2 * c + 1) * blk, :]
            hi = pltpu.roll(o4[(2 * c + 1) * blk:(2 * c + 2) * blk, :], HEAD_DIM, 1)
            lo = lo / (pltpu.roll(lo, HEAD_DIM, 1) + sink_terms[2 * c])
            hi = hi / (pltpu.roll(hi, HEAD_DIM, 1) + sink_terms[2 * c + 1])
            attn_ref[rows, (2 * kh + c) * V7X_LANES:(2 * kh + c + 1) * V7X_LANES] = (
                jnp.where(low_head, lo, hi).astype(BF16))
        gc = slice(n * g_cols, (n + 1) * g_cols)
        g_ref[:, gc] = jnp.dot(hext_ref[HALO:HALO + tile, :], wg_ref[:, gc], preferred_element_type=F32)
        if n % 2 == 1 and n // 2 < len(POOL_WINDOWS):
            pool_group(n // 2)
    for gi in range(len(instances) // 2, len(POOL_WINDOWS)):
        pool_group(gi)

    a = jnp.dot(attn_ref[...], wab_ref[...], preferred_element_type=F32)

    p_branch = jnp.dot(mixed_ref[...], wpb_ref[...], preferred_element_type=F32)

    merged = (_sigmoid(g_ref[:, 0:D_MODEL]) * a + _sigmoid(g_ref[:, D_MODEL:2 * D_MODEL]) * p_branch).astype(BF16)
    y = jnp.dot(merged, wout_ref[...], preferred_element_type=F32)
    o_ref[0] = xm_ref[0] + _rms_scale(y) * gain_post


def _rope_tables(seq, tile):
    half = HEAD_DIM // 2
    inv_freq = ROPE_THETA ** (-jnp.arange(half, dtype=F32) / half)
    pos = jnp.arange(-HALO, seq + HALO)
    ang = pos.astype(F32)[:, None] * inv_freq[None, :]
    sign = jnp.where((jnp.arange(V7X_LANES) % HEAD_DIM) < half, -1.0, 1.0).astype(F32)
    cos = jnp.tile(jnp.cos(ang), (1, V7X_LANES // half))
    sin = jnp.tile(jnp.sin(ang), (1, V7X_LANES // half)) * sign
    ext = tile + 2 * HALO
    windows = lambda t: jnp.stack([t[i * tile:i * tile + ext] for i in range(seq // tile)])
    return windows(cos), windows(sin)


def _mixer_sublayer(x, mod, norm_pre, norm_post, sink, wq, wkv, wu, wg, wab, poolw, pscale, wpb, wout, sub):
    batch, seq, _ = x.shape
    tile = min(MIXER_TILE, seq)
    ext = tile + 2 * HALO
    per_tile = tile // HALO
    last_halo = seq // HALO - 1
    cos, sin = _rope_tables(seq, tile)
    return pl.pallas_call(
        functools.partial(_mixer_kernel, sub=sub, seq=seq),
        grid=(batch, seq // tile),
        in_specs=[
            pl.BlockSpec((1, tile, D_MODEL), lambda b, i: (b, i, 0)),
            pl.BlockSpec((1, HALO, D_MODEL), lambda b, i: (b, jnp.maximum(i * per_tile - 1, 0), 0)),
            pl.BlockSpec((1, HALO, D_MODEL), lambda b, i: (b, jnp.minimum((i + 1) * per_tile, last_halo), 0)),
            pl.BlockSpec((1, 3 * N_SUBLAYERS, D_MODEL), lambda b, i: (b, 0, 0)),
            _resident((N_SUBLAYERS, D_MODEL)),
            _resident((N_SUBLAYERS, D_MODEL)),
            pl.BlockSpec((1, ext, V7X_LANES), lambda b, i: (i, 0, 0)),
            pl.BlockSpec((1, ext, V7X_LANES), lambda b, i: (i, 0, 0)),
            pl.BlockSpec(memory_space=pltpu.SMEM),
            _resident(wq.shape),
            _resident(wkv.shape),
            _resident(wu.shape),
            _resident(wg.shape),
            _resident(wab.shape),
            _resident(poolw.shape),
            _resident(pscale.shape),
            _resident(wpb.shape),
            _resident(wout.shape),
        ],
        out_specs=pl.BlockSpec((1, tile, D_MODEL), lambda b, i: (b, i, 0)),
        out_shape=jax.ShapeDtypeStruct(x.shape, F32),
        scratch_shapes=[
            pltpu.VMEM((ext, D_MODEL), BF16),
            pltpu.VMEM((tile, ATTN_WIDTH), BF16),
            pltpu.VMEM((tile, ATTN_WIDTH), BF16),
            pltpu.VMEM((N_KV_HEADS, ext, V7X_LANES), BF16),
            pltpu.VMEM((N_KV_HEADS, ext, V7X_LANES), BF16),
            pltpu.VMEM((ext, POOL_WIDTH), F32),
            pltpu.VMEM((tile, 2 * D_MODEL), F32),
            pltpu.VMEM((tile, ATTN_WIDTH), BF16),
            pltpu.VMEM((tile, POOL_WIDTH), BF16),
            pltpu.VMEM((tile + 8, POOL_GROUP_DIM), F32),
            pltpu.VMEM((tile + 16, POOL_GROUP_DIM), F32),
            pltpu.VMEM((tile + 8, POOL_GROUP_DIM), F32),
            pltpu.VMEM((tile + 24, POOL_GROUP_DIM), F32),
            pltpu.VMEM((tile + 16, POOL_GROUP_DIM), F32),
            pltpu.VMEM((tile + 8, POOL_GROUP_DIM), F32),
        ],
        compiler_params=pltpu.CompilerParams(dimension_semantics=("arbitrary", "arbitrary"),
                                             vmem_limit_bytes=VMEM_LIMIT),
        name=f"mixer_sublayer{sub}",
    )(x, x, x, mod, norm_pre, norm_post, cos, sin, sink, wq, wkv, wu, wg, wab, poolw, pscale, wpb, wout)


def _encoder_layer(xs, mods, norm_pre, norm_post, ffn1, mixer, ffn2):
    outs = []
    for x, mod in zip(xs, mods):
        x = _ffn_sublayer(x, mod, norm_pre, norm_post, *ffn1, sub=0)
        x = _mixer_sublayer(x, mod, norm_pre, norm_post, *mixer, sub=1)
        x = _ffn_sublayer(x, mod, norm_pre, norm_post, *ffn2, sub=2)
        outs.append(x)
    return outs


def kernel(x_prompt, x_sample, c_prompt, c_sample, ada_w, ada_b, norm_pre, norm_post, ffn1_w_gate, ffn1_w_up, ffn1_w_down, w_in, attn_sink, w_attn_branch, pool_w, pool_scale, w_pool_branch, w_out, ffn2_w_gate, ffn2_w_up, ffn2_w_down):
    xs = [x_prompt, x_sample]
    n_prompt, n_sample = c_prompt.shape[0], c_sample.shape[0]
    rows = -(-(n_prompt + n_sample) // V7X_SUBLANES) * V7X_SUBLANES
    c_all = jnp.concatenate([c_prompt, c_sample, jnp.zeros((rows - n_prompt - n_sample, D_MODEL), F32)], axis=0)
    e_q, e_v, e_u = ATTN_WIDTH, ATTN_WIDTH + 2 * KV_WIDTH, ATTN_WIDTH + 2 * KV_WIDTH + POOL_WIDTH
    for l in range(ada_w.shape[0]):
        mod = _ada_mod(c_all, ada_w[l], ada_b[l]).reshape(rows, 3 * N_SUBLAYERS, D_MODEL)
        mods = [mod[:n_prompt], mod[n_prompt:n_prompt + n_sample]]
        ffn1 = (ffn1_w_gate[l].astype(BF16), ffn1_w_up[l].astype(BF16), ffn1_w_down[l].astype(BF16))
        ffn2 = (ffn2_w_gate[l].astype(BF16), ffn2_w_up[l].astype(BF16), ffn2_w_down[l].astype(BF16))
        w_in_l = w_in[l].astype(BF16)
        mixer = (
            attn_sink[l],
            w_in_l[:, :e_q],
            w_in_l[:, e_q:e_v],
            w_in_l[:, e_v:e_u],
            w_in_l[:, e_u:],
            w_attn_branch[l].astype(BF16),
            pool_w[l].astype(BF16),
            pool_scale[l].reshape(1, POOL_WIDTH),
            w_pool_branch[l].astype(BF16),
            w_out[l].astype(BF16),
        )
        xs = _encoder_layer(xs, mods, norm_pre[l], norm_post[l], ffn1, mixer, ffn2)
    return (xs[0], xs[1])
```

```python
import functools

import jax
import jax.numpy as jnp
from jax import lax
from jax.experimental import pallas as pl
from jax.experimental.pallas import tpu as pltpu

D_MODEL = 1024
N_HEADS = 8
N_KV_HEADS = 2
HEAD_DIM = 64
GROUP = N_HEADS // N_KV_HEADS
ATTN_WIDTH = N_HEADS * HEAD_DIM
KV_WIDTH = N_KV_HEADS * HEAD_DIM
WINDOW = 128
ROPE_THETA = 10000.0
POOL_WINDOWS = (2, 4, 8, 16)
POOL_GROUP_DIM = 128
POOL_WIDTH = len(POOL_WINDOWS) * POOL_GROUP_DIM
D_FF = 2816
N_SUBLAYERS = 3
FFN_RES_WEIGHT = 0.5
EPS = 1e-6
NEG_INF = -1e30
F32_MAX = 3.4028234663852886e38

V7X_LANES = 128
V7X_SUBLANES = 8
V7X_VMEM_BYTES = 64 * 1024 * 1024

HALO = WINDOW
FFN_TILE = 1024
FFN_SUBTILE = 256
MIXER_TILE = 512
ADA_COLS = 1536
VMEM_LIMIT = V7X_VMEM_BYTES - 8 * 1024 * 1024

F32 = jnp.float32
BF16 = jnp.bfloat16


def _sigmoid(x):
    return 1.0 / (1.0 + jnp.exp(-x))


def _rms_scale(x):
    return x * lax.rsqrt(jnp.mean(x * x, axis=-1, keepdims=True) + EPS)


def _resident(shape):
    return pl.BlockSpec(shape, lambda *_: (0,) * len(shape), pipeline_mode=pl.Buffered(1))


def _ada_kernel(c_ref, w_ref, b_ref, o_ref):
    c = c_ref[...]
    s = c * _sigmoid(c)
    w = w_ref[...]
    s_hi = s.astype(BF16)
    s_lo = (s - s_hi.astype(F32)).astype(BF16)
    w_hi = w.astype(BF16)
    w_lo = (w - w_hi.astype(F32)).astype(BF16)
    rows = s.shape[0]
    by_hi = jnp.dot(jnp.concatenate([s_hi, s_lo], axis=0), w_hi, preferred_element_type=F32)
    o_ref[...] = (jnp.dot(s_hi, w_lo, preferred_element_type=F32) + by_hi[rows:]) + by_hi[:rows] + b_ref[...]


def _ada_mod(c, ada_w, ada_b):
    rows, width = c.shape[0], ada_w.shape[1]
    return pl.pallas_call(
        _ada_kernel,
        grid=(width // ADA_COLS,),
        in_specs=[
            pl.BlockSpec((rows, D_MODEL), lambda n: (0, 0)),
            pl.BlockSpec((D_MODEL, ADA_COLS), lambda n: (0, n)),
            pl.BlockSpec((1, ADA_COLS), lambda n: (0, n)),
        ],
        out_specs=pl.BlockSpec((rows, ADA_COLS), lambda n: (0, n)),
        out_shape=jax.ShapeDtypeStruct((rows, width), F32),
        compiler_params=pltpu.CompilerParams(dimension_semantics=("arbitrary",),
                                             vmem_limit_bytes=VMEM_LIMIT),
        name="ada_mod",
    )(c, ada_w, ada_b.reshape(1, width))


def _ffn_kernel(x_ref, mod_ref, gpre_ref, gpost_ref, wg_ref, wu_ref, wd_ref, o_ref, *, sub):
    shift = mod_ref[0, 3 * sub:3 * sub + 1, :]
    scl = mod_ref[0, 3 * sub + 1:3 * sub + 2, :]
    gate = mod_ref[0, 3 * sub + 2:3 * sub + 3, :]
    gain_pre = gpre_ref[sub:sub + 1, :] * (1.0 + scl)
    gain_post = gpost_ref[sub:sub + 1, :] * (FFN_RES_WEIGHT * gate)
    n_sub = x_ref.shape[1] // FFN_SUBTILE

    def rows(k):
        return slice(k * FFN_SUBTILE, (k + 1) * FFN_SUBTILE)

    def pre(k):
        return (_rms_scale(x_ref[0, rows(k), :]) * gain_pre + shift).astype(BF16)

    def post(k, y):
        o_ref[0, rows(k), :] = x_ref[0, rows(k), :] + _rms_scale(y) * gain_post

    h = pre(0)
    y_prev = None
    for k in range(n_sub):
        g = jnp.dot(h, wg_ref[...], preferred_element_type=F32)
        u = jnp.dot(h, wu_ref[...], preferred_element_type=F32)
        if k + 1 < n_sub:
            h = pre(k + 1)
        if y_prev is not None:
            post(k - 1, y_prev)
        a = (g * _sigmoid(g) * u).astype(BF16)
        y_prev = jnp.dot(a, wd_ref[...], preferred_element_type=F32)
    post(n_sub - 1, y_prev)


def _ffn_sublayer(x, mod, norm_pre, norm_post, wg, wu, wd, sub):
    batch, seq, _ = x.shape
    tile = min(FFN_TILE, seq)
    return pl.pallas_call(
        functools.partial(_ffn_kernel, sub=sub),
        grid=(batch, seq // tile),
        in_specs=[
            pl.BlockSpec((1, tile, D_MODEL), lambda b, i: (b, i, 0)),
            pl.BlockSpec((1, 3 * N_SUBLAYERS, D_MODEL), lambda b, i: (b, 0, 0)),
            _resident((N_SUBLAYERS, D_MODEL)),
            _resident((N_SUBLAYERS, D_MODEL)),
            _resident((D_MODEL, D_FF)),
            _resident((D_MODEL, D_FF)),
            _resident((D_FF, D_MODEL)),
        ],
        out_specs=pl.BlockSpec((1, tile, D_MODEL), lambda b, i: (b, i, 0)),
        out_shape=jax.ShapeDtypeStruct(x.shape, F32),
        compiler_params=pltpu.CompilerParams(dimension_semantics=("arbitrary", "arbitrary"),
                                             vmem_limit_bytes=VMEM_LIMIT),
        name=f"ffn_sublayer{sub}",
    )(x, mod, norm_pre, norm_post, wg, wu, wd)


def _rope(x, cos, sin_signed, first_half):
    partner = jnp.where(first_half, pltpu.roll(x, V7X_LANES - HEAD_DIM // 2, 1),
                        pltpu.roll(x, HEAD_DIM // 2, 1))
    return x * cos + partner * sin_signed


def _mixer_kernel(xm_ref, xl_ref, xr_ref, mod_ref, gpre_ref, gpost_ref, cos_ref, sin_ref, sink_ref,
                  win_ref, wab_ref, poolw_ref, pscale_ref, wpb_ref, wout_ref,
                  o_ref,
                  hext_ref, qlo_ref, qhi_ref, k_ref, v_ref, u_ref, g_ref, attn_ref, mixed_ref,
                  c1_ref, c2_ref, b2_ref, c3_ref, b3_ref, a3_ref, *, sub, seq):
    tile = xm_ref.shape[1]
    ext = tile + 2 * HALO
    blk = WINDOW
    n_blk = tile // blk
    i = pl.program_id(1)
    n_tiles = pl.num_programs(1)
    shift = mod_ref[0, 3 * sub:3 * sub + 1, :]
    scl = mod_ref[0, 3 * sub + 1:3 * sub + 2, :]
    gate = mod_ref[0, 3 * sub + 2:3 * sub + 3, :]
    gain_pre = gpre_ref[sub:sub + 1, :] * (1.0 + scl)
    gain_post = gpost_ref[sub:sub + 1, :] * gate
    e_q, e_v, e_u = ATTN_WIDTH, ATTN_WIDTH + 2 * KV_WIDTH, ATTN_WIDTH + 2 * KV_WIDTH + POOL_WIDTH

    def pre(x):
        return (_rms_scale(x) * gain_pre + shift).astype(BF16)

    hext_ref[0:HALO, :] = pre(xl_ref[0])
    hext_ref[HALO:HALO + tile, :] = pre(xm_ref[0])
    hext_ref[HALO + tile:ext, :] = pre(xr_ref[0])

    lane = lax.broadcasted_iota(jnp.int32, (1, V7X_LANES), 1)
    first_half = (lane % HEAD_DIM) < (HEAD_DIM // 2)
    low_head = lane < HEAD_DIM
    cos = cos_ref[0]
    sin = sin_ref[0]

    kv = jnp.dot(hext_ref[...], win_ref[:, e_q:e_v], preferred_element_type=F32)
    k = _rope(kv[:, 0:KV_WIDTH], cos, sin, first_half)
    v = kv[:, KV_WIDTH:2 * KV_WIDTH]
    k_sw = pltpu.roll(k, HEAD_DIM, 1)
    v_sw = pltpu.roll(v, HEAD_DIM, 1)
    k_ref[0] = jnp.where(low_head, k, k_sw).astype(BF16)
    k_ref[1] = jnp.where(low_head, k_sw, k).astype(BF16)
    v_ref[0] = jnp.where(low_head, v, 1.0).astype(BF16)
    v_ref[1] = jnp.where(low_head, v_sw, 1.0).astype(BF16)

    q = jnp.dot(hext_ref[HALO:HALO + tile, :], win_ref[:, 0:e_q], preferred_element_type=F32)
    cos_m = cos[HALO:HALO + tile, :]
    sin_m = sin[HALO:HALO + tile, :]
    for cb in range(ATTN_WIDTH // V7X_LANES):
        cols = slice(cb * V7X_LANES, (cb + 1) * V7X_LANES)
        qc = _rope(q[:, cols], cos_m, sin_m, first_half) * (HEAD_DIM ** -0.5)
        qlo_ref[:, cols] = jnp.where(low_head, qc, 0.0).astype(BF16)
        qhi_ref[:, cols] = jnp.where(low_head, 0.0, qc).astype(BF16)

    u_ext = jnp.dot(hext_ref[...], win_ref[:, e_v:e_u], preferred_element_type=F32)
    u_ref[0:HALO, :] = jnp.where(i > 0, u_ext[0:HALO, :], 0.0)
    u_ref[HALO:HALO + tile, :] = u_ext[HALO:HALO + tile, :]
    u_ref[HALO + tile:ext, :] = jnp.where(i < n_tiles - 1, u_ext[HALO + tile:ext, :], 0.0)

    base = HALO - V7X_SUBLANES
    pos = i * tile + lax.broadcasted_iota(jnp.int32, (tile, 1), 0)
    levels = ((), (c1_ref,), (c2_ref, b2_ref), (c3_ref, b3_ref, a3_ref))

    def pool_group(gi):
        w = POOL_WINDOWS[gi]
        cols = slice(gi * POOL_GROUP_DIM, (gi + 1) * POOL_GROUP_DIM)
        if not levels[gi]:
            acc = u_ref[HALO - 1:HALO - 1 + tile, cols] + u_ref[HALO:HALO + tile, cols]
        else:
            n0 = levels[gi][0].shape[0]
            levels[gi][0][...] = u_ref[base:base + n0, cols] + u_ref[base + 1:base + 1 + n0, cols]
            for lv in range(1, len(levels[gi])):
                n = levels[gi][lv].shape[0]
                step = 2 ** lv
                levels[gi][lv][...] = levels[gi][lv - 1][0:n, :] + levels[gi][lv - 1][step:step + n, :]
            top = levels[gi][-1]
            first = V7X_SUBLANES - w // 2
            acc = top[first:first + tile, :] + top[V7X_SUBLANES:V7X_SUBLANES + tile, :]
        cnt = jnp.minimum(pos + w // 2, seq) - jnp.maximum(pos - w // 2, 0)
        pooled = acc / cnt.astype(F32) - u_ref[HALO:HALO + tile, cols]
        mixed = jnp.dot(pooled.astype(BF16), poolw_ref[gi], preferred_element_type=F32)
        mixed_ref[:, cols] = (mixed * pscale_ref[:, cols]).astype(BF16)

    r = lax.broadcasted_iota(jnp.int32, (blk, 3 * blk), 0)
    jj = lax.broadcasted_iota(jnp.int32, (blk, 3 * blk), 1)
    cap_band = jnp.where((jj >= r) & (jj <= r + 2 * WINDOW), F32_MAX, NEG_INF)
    cap_first = jnp.where((jj < blk) & (i == 0), NEG_INF, cap_band)
    cap_last = jnp.where((jj >= 2 * blk) & (i == n_tiles - 1), NEG_INF, cap_band)

    g_cols = 2 * D_MODEL // (n_blk * N_KV_HEADS)
    instances = [(b, kh) for b in range(n_blk) for kh in range(N_KV_HEADS)]

    def scores(b, kh):
        rows = slice(b * blk, (b + 1) * blk)
        q4 = jnp.concatenate(
            [ref[rows, (2 * kh + c) * V7X_LANES:(2 * kh + c + 1) * V7X_LANES]
             for c in range(GROUP // 2) for ref in (qlo_ref, qhi_ref)], axis=0)
        kw = k_ref[kh, b * blk:b * blk + 3 * blk, :]
        return lax.dot_general(q4, kw, (((1,), (1,)), ((), ())), preferred_element_type=F32)

    s_next = scores(*instances[0])
    for n, (b, kh) in enumerate(instances):
        rows = slice(b * blk, (b + 1) * blk)
        cap = cap_band
        if b == 0:
            cap = cap_first
        if b == n_blk - 1:
            cap = jnp.minimum(cap, cap_last)
        s = s_next
        if n + 1 < len(instances):
            s_next = scores(*instances[n + 1])
        vw = v_ref[kh, b * blk:b * blk + 3 * blk, :]
        ps, sink_terms = [], []
        for gi in range(GROUP):
            sink = sink_ref[GROUP * kh + gi]
            sh = s[gi * blk:(gi + 1) * blk, :]
            sg = jnp.concatenate([jnp.minimum(sh[:, :blk], cap[:, :blk]), sh[:, blk:2 * blk],
                                  jnp.minimum(sh[:, 2 * blk:], cap[:, 2 * blk:])], axis=1)
            m = jnp.maximum(jnp.max(sg, axis=-1, keepdims=True), sink)
            ps.append(jnp.exp(sg - m).astype(BF16))
            sink_terms.append(jnp.exp(sink - m))
        o4 = jnp.dot(jnp.concatenate(ps, axis=0), vw, preferred_element_type=F32)
        for c in range(GROUP // 2):
            lo = o4[(2 * c) * blk:(2 * c + 1) * blk, :]
            hi = pltpu.roll(o4[(2 * c + 1) * blk:(2 * c + 2) * blk, :], HEAD_DIM, 1)
            lo = lo / (pltpu.roll(lo, HEAD_DIM, 1) + sink_terms[2 * c])
            hi = hi / (pltpu.roll(hi, HEAD_DIM, 1) + sink_terms[2 * c + 1])
            attn_ref[rows, (2 * kh + c) * V7X_LANES:(2 * kh + c + 1) * V7X_LANES] = (
                jnp.where(low_head, lo, hi).astype(BF16))
        gc = slice(n * g_cols, (n + 1) * g_cols)
        g_ref[:, gc] = jnp.dot(hext_ref[HALO:HALO + tile, :], win_ref[:, e_u + n * g_cols:e_u + (n + 1) * g_cols],
                               preferred_element_type=F32)
        if n % 2 == 1 and n // 2 < len(POOL_WINDOWS):
            pool_group(n // 2)
    for gi in range(len(instances) // 2, len(POOL_WINDOWS)):
        pool_group(gi)

    a = jnp.dot(attn_ref[...], wab_ref[...], preferred_element_type=F32)

    p_branch = jnp.dot(mixed_ref[...], wpb_ref[...], preferred_element_type=F32)

    merged = (_sigmoid(g_ref[:, 0:D_MODEL]) * a + _sigmoid(g_ref[:, D_MODEL:2 * D_MODEL]) * p_branch).astype(BF16)
    y = jnp.dot(merged, wout_ref[...], preferred_element_type=F32)
    o_ref[0] = xm_ref[0] + _rms_scale(y) * gain_post


def _rope_tables(seq, tile):
    half = HEAD_DIM // 2
    inv_freq = ROPE_THETA ** (-jnp.arange(half, dtype=F32) / half)
    pos = (jnp.arange(seq // tile) * tile - HALO)[:, None] + jnp.arange(tile + 2 * HALO)[None, :]
    ang = pos.astype(F32)[:, :, None] * inv_freq[None, None, :]
    sign = jnp.where((jnp.arange(V7X_LANES) % HEAD_DIM) < half, -1.0, 1.0).astype(F32)
    reps = (1, 1, V7X_LANES // half)
    return jnp.tile(jnp.cos(ang), reps), jnp.tile(jnp.sin(ang), reps) * sign


def _mixer_sublayer(x, mod, norm_pre, norm_post, sink, win, wab, poolw, pscale, wpb, wout, sub):
    batch, seq, _ = x.shape
    tile = min(MIXER_TILE, seq)
    ext = tile + 2 * HALO
    per_tile = tile // HALO
    last_halo = seq // HALO - 1
    cos, sin = _rope_tables(seq, tile)
    return pl.pallas_call(
        functools.partial(_mixer_kernel, sub=sub, seq=seq),
        grid=(batch, seq // tile),
        in_specs=[
            pl.BlockSpec((1, tile, D_MODEL), lambda b, i: (b, i, 0)),
            pl.BlockSpec((1, HALO, D_MODEL), lambda b, i: (b, jnp.maximum(i * per_tile - 1, 0), 0)),
            pl.BlockSpec((1, HALO, D_MODEL), lambda b, i: (b, jnp.minimum((i + 1) * per_tile, last_halo), 0)),
            pl.BlockSpec((1, 3 * N_SUBLAYERS, D_MODEL), lambda b, i: (b, 0, 0)),
            _resident((N_SUBLAYERS, D_MODEL)),
            _resident((N_SUBLAYERS, D_MODEL)),
            pl.BlockSpec((1, ext, V7X_LANES), lambda b, i: (i, 0, 0)),
            pl.BlockSpec((1, ext, V7X_LANES), lambda b, i: (i, 0, 0)),
            pl.BlockSpec(memory_space=pltpu.SMEM),
            _resident(win.shape),
            _resident(wab.shape),
            _resident(poolw.shape),
            _resident(pscale.shape),
            _resident(wpb.shape),
            _resident(wout.shape),
        ],
        out_specs=pl.BlockSpec((1, tile, D_MODEL), lambda b, i: (b, i, 0)),
        out_shape=jax.ShapeDtypeStruct(x.shape, F32),
        scratch_shapes=[
            pltpu.VMEM((ext, D_MODEL), BF16),
            pltpu.VMEM((tile, ATTN_WIDTH), BF16),
            pltpu.VMEM((tile, ATTN_WIDTH), BF16),
            pltpu.VMEM((N_KV_HEADS, ext, V7X_LANES), BF16),
            pltpu.VMEM((N_KV_HEADS, ext, V7X_LANES), BF16),
            pltpu.VMEM((ext, POOL_WIDTH), F32),
            pltpu.VMEM((tile, 2 * D_MODEL), F32),
            pltpu.VMEM((tile, ATTN_WIDTH), BF16),
            pltpu.VMEM((tile, POOL_WIDTH), BF16),
            pltpu.VMEM((tile + 8, POOL_GROUP_DIM), F32),
            pltpu.VMEM((tile + 16, POOL_GROUP_DIM), F32),
            pltpu.VMEM((tile + 8, POOL_GROUP_DIM), F32),
            pltpu.VMEM((tile + 24, POOL_GROUP_DIM), F32),
            pltpu.VMEM((tile + 16, POOL_GROUP_DIM), F32),
            pltpu.VMEM((tile + 8, POOL_GROUP_DIM), F32),
        ],
        compiler_params=pltpu.CompilerParams(dimension_semantics=("arbitrary", "arbitrary"),
                                             vmem_limit_bytes=VMEM_LIMIT),
        name=f"mixer_sublayer{sub}",
    )(x, x, x, mod, norm_pre, norm_post, cos, sin, sink, win, wab, poolw, pscale, wpb, wout)


def _encoder_layer(xs, mods, norm_pre, norm_post, ffn1, mixer, ffn2):
    outs = []
    for x, mod in zip(xs, mods):
        x = _ffn_sublayer(x, mod, norm_pre, norm_post, *ffn1, sub=0)
        x = _mixer_sublayer(x, mod, norm_pre, norm_post, *mixer, sub=1)
        x = _ffn_sublayer(x, mod, norm_pre, norm_post, *ffn2, sub=2)
        outs.append(x)
    return outs


def kernel(x_prompt, x_sample, c_prompt, c_sample, ada_w, ada_b, norm_pre, norm_post, ffn1_w_gate, ffn1_w_up, ffn1_w_down, w_in, attn_sink, w_attn_branch, pool_w, pool_scale, w_pool_branch, w_out, ffn2_w_gate, ffn2_w_up, ffn2_w_down):
    xs = [x_prompt, x_sample]
    n_prompt, n_sample = c_prompt.shape[0], c_sample.shape[0]
    rows = -(-(n_prompt + n_sample) // V7X_SUBLANES) * V7X_SUBLANES
    c_all = jnp.concatenate([c_prompt, c_sample, jnp.zeros((rows - n_prompt - n_sample, D_MODEL), F32)], axis=0)
    for l in range(ada_w.shape[0]):
        mod = _ada_mod(c_all, ada_w[l], ada_b[l]).reshape(rows, 3 * N_SUBLAYERS, D_MODEL)
        mods = [mod[:n_prompt], mod[n_prompt:n_prompt + n_sample]]
        ffn1 = (ffn1_w_gate[l].astype(BF16), ffn1_w_up[l].astype(BF16), ffn1_w_down[l].astype(BF16))
        ffn2 = (ffn2_w_gate[l].astype(BF16), ffn2_w_up[l].astype(BF16), ffn2_w_down[l].astype(BF16))
        mixer = (
            attn_sink[l],
            w_in[l].astype(BF16),
            w_attn_branch[l].astype(BF16),
            pool_w[l].astype(BF16),
            pool_scale[l].reshape(1, POOL_WIDTH),
            w_pool_branch[l].astype(BF16),
            w_out[l].astype(BF16),
        )
        xs = _encoder_layer(xs, mods, norm_pre[l], norm_post[l], ffn1, mixer, ffn2)
    return (xs[0], xs[1])
```

```python
import functools

import jax
import jax.numpy as jnp
from jax import lax
from jax.experimental import pallas as pl
from jax.experimental.pallas import tpu as pltpu

D_MODEL = 1024
N_HEADS = 8
N_KV_HEADS = 2
HEAD_DIM = 64
GROUP = N_HEADS // N_KV_HEADS
ATTN_WIDTH = N_HEADS * HEAD_DIM
KV_WIDTH = N_KV_HEADS * HEAD_DIM
WINDOW = 128
ROPE_THETA = 10000.0
POOL_WINDOWS = (2, 4, 8, 16)
POOL_GROUP_DIM = 128
POOL_WIDTH = len(POOL_WINDOWS) * POOL_GROUP_DIM
D_FF = 2816
N_SUBLAYERS = 3
FFN_RES_WEIGHT = 0.5
EPS = 1e-6
NEG_INF = -1e30
F32_MAX = 3.4028234663852886e38

V7X_LANES = 128
V7X_SUBLANES = 8
V7X_VMEM_BYTES = 64 * 1024 * 1024

HALO = WINDOW
FFN_TILE = 1024
FFN_SUBTILE = 256
MIXER_TILE = 512
ADA_COLS = 1536
VMEM_LIMIT = V7X_VMEM_BYTES - 8 * 1024 * 1024

F32 = jnp.float32
BF16 = jnp.bfloat16


def _sigmoid(x):
    return 1.0 / (1.0 + jnp.exp(-x))


def _rms_scale(x):
    return x * lax.rsqrt(jnp.mean(x * x, axis=-1, keepdims=True) + EPS)


def _resident(shape):
    return pl.BlockSpec(shape, lambda *_: (0,) * len(shape), pipeline_mode=pl.Buffered(1))


def _ada_kernel(c_ref, w_ref, b_ref, o_ref):
    c = c_ref[...]
    s = c * _sigmoid(c)
    w = w_ref[...]
    s_hi = s.astype(BF16)
    s_lo = (s - s_hi.astype(F32)).astype(BF16)
    w_hi = w.astype(BF16)
    w_lo = (w - w_hi.astype(F32)).astype(BF16)
    rows = s.shape[0]
    by_hi = jnp.dot(jnp.concatenate([s_hi, s_lo], axis=0), w_hi, preferred_element_type=F32)
    o_ref[...] = (jnp.dot(s_hi, w_lo, preferred_element_type=F32) + by_hi[rows:]) + by_hi[:rows] + b_ref[...]


def _ada_mod(c, ada_w, ada_b):
    rows, width = c.shape[0], ada_w.shape[1]
    return pl.pallas_call(
        _ada_kernel,
        grid=(width // ADA_COLS,),
        in_specs=[
            pl.BlockSpec((rows, D_MODEL), lambda n: (0, 0)),
            pl.BlockSpec((D_MODEL, ADA_COLS), lambda n: (0, n)),
            pl.BlockSpec((1, ADA_COLS), lambda n: (0, n)),
        ],
        out_specs=pl.BlockSpec((rows, ADA_COLS), lambda n: (0, n)),
        out_shape=jax.ShapeDtypeStruct((rows, width), F32),
        compiler_params=pltpu.CompilerParams(dimension_semantics=("arbitrary",),
                                             vmem_limit_bytes=VMEM_LIMIT),
        name="ada_mod",
    )(c, ada_w, ada_b.reshape(1, width))


def _ffn_kernel(x_ref, mod_ref, gpre_ref, gpost_ref, wg_ref, wu_ref, wd_ref, o_ref, *, sub):
    shift = mod_ref[0, 3 * sub:3 * sub + 1, :]
    scl = mod_ref[0, 3 * sub + 1:3 * sub + 2, :]
    gate = mod_ref[0, 3 * sub + 2:3 * sub + 3, :]
    gain_pre = gpre_ref[sub:sub + 1, :] * (1.0 + scl)
    gain_post = gpost_ref[sub:sub + 1, :] * (FFN_RES_WEIGHT * gate)
    n_sub = x_ref.shape[1] // FFN_SUBTILE

    def rows(k):
        return slice(k * FFN_SUBTILE, (k + 1) * FFN_SUBTILE)

    def pre(k):
        return (_rms_scale(x_ref[0, rows(k), :]) * gain_pre + shift).astype(BF16)

    def post(k, y):
        o_ref[0, rows(k), :] = x_ref[0, rows(k), :] + _rms_scale(y) * gain_post

    h = pre(0)
    y_prev = None
    for k in range(n_sub):
        g = jnp.dot(h, wg_ref[...], preferred_element_type=F32)
        u = jnp.dot(h, wu_ref[...], preferred_element_type=F32)
        if k + 1 < n_sub:
            h = pre(k + 1)
        if y_prev is not None:
            post(k - 1, y_prev)
        a = (g * _sigmoid(g) * u).astype(BF16)
        y_prev = jnp.dot(a, wd_ref[...], preferred_element_type=F32)
    post(n_sub - 1, y_prev)


def _ffn_sublayer(x, mod, norm_pre, norm_post, wg, wu, wd, sub):
    batch, seq, _ = x.shape
    tile = min(FFN_TILE, seq)
    return pl.pallas_call(
        functools.partial(_ffn_kernel, sub=sub),
        grid=(batch, seq // tile),
        in_specs=[
            pl.BlockSpec((1, tile, D_MODEL), lambda b, i: (b, i, 0)),
            pl.BlockSpec((1, 3 * N_SUBLAYERS, D_MODEL), lambda b, i: (b, 0, 0)),
            _resident((N_SUBLAYERS, D_MODEL)),
            _resident((N_SUBLAYERS, D_MODEL)),
            _resident((D_MODEL, D_FF)),
            _resident((D_MODEL, D_FF)),
            _resident((D_FF, D_MODEL)),
        ],
        out_specs=pl.BlockSpec((1, tile, D_MODEL), lambda b, i: (b, i, 0)),
        out_shape=jax.ShapeDtypeStruct(x.shape, F32),
        compiler_params=pltpu.CompilerParams(dimension_semantics=("arbitrary", "arbitrary"),
                                             vmem_limit_bytes=VMEM_LIMIT),
        name=f"ffn_sublayer{sub}",
    )(x, mod, norm_pre, norm_post, wg, wu, wd)


def _rope(x, cos, sin_signed, first_half):
    partner = jnp.where(first_half, pltpu.roll(x, V7X_LANES - HEAD_DIM // 2, 1),
                        pltpu.roll(x, HEAD_DIM // 2, 1))
    return x * cos + partner * sin_signed


def _mixer_kernel(xm_ref, xr_ref, mod_ref, gpre_ref, gpost_ref, cos_ref, sin_ref, sink_ref,
                  win_ref, wab_ref, poolw_ref, pscale_ref, wpb_ref, wout_ref,
                  o_ref,
                  hext_ref, qlo_ref, qhi_ref, k_ref, v_ref, u_ref, g_ref, attn_ref, mixed_ref,
                  c1_ref, c2_ref, b2_ref, c3_ref, b3_ref, a3_ref, *, sub, seq):
    tile = xm_ref.shape[1]
    ext = tile + 2 * HALO
    blk = WINDOW
    n_blk = tile // blk
    i = pl.program_id(1)
    n_tiles = pl.num_programs(1)
    shift = mod_ref[0, 3 * sub:3 * sub + 1, :]
    scl = mod_ref[0, 3 * sub + 1:3 * sub + 2, :]
    gate = mod_ref[0, 3 * sub + 2:3 * sub + 3, :]
    gain_pre = gpre_ref[sub:sub + 1, :] * (1.0 + scl)
    gain_post = gpost_ref[sub:sub + 1, :] * gate
    e_q, e_v, e_u = ATTN_WIDTH, ATTN_WIDTH + 2 * KV_WIDTH, ATTN_WIDTH + 2 * KV_WIDTH + POOL_WIDTH

    def pre(x):
        return (_rms_scale(x) * gain_pre + shift).astype(BF16)

    @pl.when(i == 0)
    def _():
        k_ref[:, 0:HALO, :] = jnp.zeros((N_KV_HEADS, HALO, V7X_LANES), BF16)
        v_ref[:, 0:HALO, :] = jnp.zeros((N_KV_HEADS, HALO, V7X_LANES), BF16)
        u_ref[0:HALO, :] = jnp.zeros((HALO, POOL_WIDTH), F32)

    @pl.when(i > 0)
    def _():
        k_ref[:, 0:HALO, :] = k_ref[:, tile:tile + HALO, :]
        v_ref[:, 0:HALO, :] = v_ref[:, tile:tile + HALO, :]
        u_ref[0:HALO, :] = u_ref[tile:tile + HALO, :]

    hext_ref[0:tile, :] = pre(xm_ref[0])
    hext_ref[tile:tile + HALO, :] = pre(xr_ref[0])

    lane = lax.broadcasted_iota(jnp.int32, (1, V7X_LANES), 1)
    first_half = (lane % HEAD_DIM) < (HEAD_DIM // 2)
    low_head = lane < HEAD_DIM
    cos = cos_ref[0]
    sin = sin_ref[0]

    kv = jnp.dot(hext_ref[...], win_ref[:, e_q:e_v], preferred_element_type=F32)
    k = _rope(kv[:, 0:KV_WIDTH], cos, sin, first_half)
    v = kv[:, KV_WIDTH:2 * KV_WIDTH]
    k_sw = pltpu.roll(k, HEAD_DIM, 1)
    v_sw = pltpu.roll(v, HEAD_DIM, 1)
    k_ref[0, HALO:ext, :] = jnp.where(low_head, k, k_sw).astype(BF16)
    k_ref[1, HALO:ext, :] = jnp.where(low_head, k_sw, k).astype(BF16)
    v_ref[0, HALO:ext, :] = jnp.where(low_head, v, 1.0).astype(BF16)
    v_ref[1, HALO:ext, :] = jnp.where(low_head, v_sw, 1.0).astype(BF16)

    q = jnp.dot(hext_ref[0:tile, :], win_ref[:, 0:e_q], preferred_element_type=F32)
    cos_m = cos[0:tile, :]
    sin_m = sin[0:tile, :]
    for cb in range(ATTN_WIDTH // V7X_LANES):
        cols = slice(cb * V7X_LANES, (cb + 1) * V7X_LANES)
        qc = _rope(q[:, cols], cos_m, sin_m, first_half) * (HEAD_DIM ** -0.5)
        qlo_ref[:, cols] = jnp.where(low_head, qc, 0.0).astype(BF16)
        qhi_ref[:, cols] = jnp.where(low_head, 0.0, qc).astype(BF16)

    u_new = jnp.dot(hext_ref[...], win_ref[:, e_v:e_u], preferred_element_type=F32)
    u_ref[HALO:HALO + tile, :] = u_new[0:tile, :]
    u_ref[HALO + tile:ext, :] = jnp.where(i < n_tiles - 1, u_new[tile:tile + HALO, :], 0.0)

    base = HALO - V7X_SUBLANES
    pos = i * tile + lax.broadcasted_iota(jnp.int32, (tile, 1), 0)
    levels = ((), (c1_ref,), (c2_ref, b2_ref), (c3_ref, b3_ref, a3_ref))

    def pool_group(gi):
        w = POOL_WINDOWS[gi]
        cols = slice(gi * POOL_GROUP_DIM, (gi + 1) * POOL_GROUP_DIM)
        if not levels[gi]:
            acc = u_ref[HALO - 1:HALO - 1 + tile, cols] + u_ref[HALO:HALO + tile, cols]
        else:
            n0 = levels[gi][0].shape[0]
            levels[gi][0][...] = u_ref[base:base + n0, cols] + u_ref[base + 1:base + 1 + n0, cols]
            for lv in range(1, len(levels[gi])):
                n = levels[gi][lv].shape[0]
                step = 2 ** lv
                levels[gi][lv][...] = levels[gi][lv - 1][0:n, :] + levels[gi][lv - 1][step:step + n, :]
            top = levels[gi][-1]
            first = V7X_SUBLANES - w // 2
            acc = top[first:first + tile, :] + top[V7X_SUBLANES:V7X_SUBLANES + tile, :]
        cnt = jnp.minimum(pos + w // 2, seq) - jnp.maximum(pos - w // 2, 0)
        pooled = acc / cnt.astype(F32) - u_ref[HALO:HALO + tile, cols]
        mixed = jnp.dot(pooled.astype(BF16), poolw_ref[gi], preferred_element_type=F32)
        mixed_ref[:, cols] = (mixed * pscale_ref[:, cols]).astype(BF16)

    r = lax.broadcasted_iota(jnp.int32, (blk, 3 * blk), 0)
    jj = lax.broadcasted_iota(jnp.int32, (blk, 3 * blk), 1)
    cap_band = jnp.where((jj >= r) & (jj <= r + 2 * WINDOW), F32_MAX, NEG_INF)
    cap_first = jnp.where((jj < blk) & (i == 0), NEG_INF, cap_band)
    cap_last = jnp.where((jj >= 2 * blk) & (i == n_tiles - 1), NEG_INF, cap_band)

    g_cols = 2 * D_MODEL // (n_blk * N_KV_HEADS)
    instances = [(b, kh) for b in range(n_blk) for kh in range(N_KV_HEADS)]

    def scores(b, kh):
        rows = slice(b * blk, (b + 1) * blk)
        q4 = jnp.concatenate(
            [ref[rows, (2 * kh + c) * V7X_LANES:(2 * kh + c + 1) * V7X_LANES]
             for c in range(GROUP // 2) for ref in (qlo_ref, qhi_ref)], axis=0)
        kw = k_ref[kh, b * blk:b * blk + 3 * blk, :]
        return lax.dot_general(q4, kw, (((1,), (1,)), ((), ())), preferred_element_type=F32)

    s_next = scores(*instances[0])
    for n, (b, kh) in enumerate(instances):
        rows = slice(b * blk, (b + 1) * blk)
        cap = cap_band
        if b == 0:
            cap = cap_first
        if b == n_blk - 1:
            cap = jnp.minimum(cap, cap_last)
        s = s_next
        if n + 1 < len(instances):
            s_next = scores(*instances[n + 1])
        vw = v_ref[kh, b * blk:b * blk + 3 * blk, :]
        ps, sink_terms = [], []
        for gi in range(GROUP):
            sink = sink_ref[GROUP * kh + gi]
            sh = s[gi * blk:(gi + 1) * blk, :]
            sg = jnp.concatenate([jnp.minimum(sh[:, :blk], cap[:, :blk]), sh[:, blk:2 * blk],
                                  jnp.minimum(sh[:, 2 * blk:], cap[:, 2 * blk:])], axis=1)
            m = jnp.maximum(jnp.max(sg, axis=-1, keepdims=True), sink)
            ps.append(jnp.exp(sg - m).astype(BF16))
            sink_terms.append(jnp.exp(sink - m))
        o4 = jnp.dot(jnp.concatenate(ps, axis=0), vw, preferred_element_type=F32)
        for c in range(GROUP // 2):
            lo = o4[(2 * c) * blk:(2 * c + 1) * blk, :]
            hi = pltpu.roll(o4[(2 * c + 1) * blk:(2 * c + 2) * blk, :], HEAD_DIM, 1)
            lo = lo / (pltpu.roll(lo, HEAD_DIM, 1) + sink_terms[2 * c])
            hi = hi / (pltpu.roll(hi, HEAD_DIM, 1) + sink_terms[2 * c + 1])
            attn_ref[rows, (2 * kh + c) * V7X_LANES:(2 * kh + c + 1) * V7X_LANES] = (
                jnp.where(low_head, lo, hi).astype(BF16))
        gc = slice(n * g_cols, (n + 1) * g_cols)
        g_ref[:, gc] = jnp.dot(hext_ref[0:tile, :], win_ref[:, e_u + n * g_cols:e_u + (n + 1) * g_cols],
                               preferred_element_type=F32)
        if n % 2 == 1 and n // 2 < len(POOL_WINDOWS):
            pool_group(n // 2)
    for gi in range(len(instances) // 2, len(POOL_WINDOWS)):
        pool_group(gi)

    a = jnp.dot(attn_ref[...], wab_ref[...], preferred_element_type=F32)

    p_branch = jnp.dot(mixed_ref[...], wpb_ref[...], preferred_element_type=F32)

    merged = (_sigmoid(g_ref[:, 0:D_MODEL]) * a + _sigmoid(g_ref[:, D_MODEL:2 * D_MODEL]) * p_branch).astype(BF16)
    y = jnp.dot(merged, wout_ref[...], preferred_element_type=F32)
    o_ref[0] = xm_ref[0] + _rms_scale(y) * gain_post


def _rope_tables(seq, tile):
    half = HEAD_DIM // 2
    inv_freq = ROPE_THETA ** (-jnp.arange(half, dtype=F32) / half)
    pos = (jnp.arange(seq // tile) * tile)[:, None] + jnp.arange(tile + HALO)[None, :]
    ang = pos.astype(F32)[:, :, None] * inv_freq[None, None, :]
    sign = jnp.where((jnp.arange(V7X_LANES) % HEAD_DIM) < half, -1.0, 1.0).astype(F32)
    reps = (1, 1, V7X_LANES // half)
    return jnp.tile(jnp.cos(ang), reps), jnp.tile(jnp.sin(ang), reps) * sign


def _mixer_sublayer(x, mod, norm_pre, norm_post, sink, win, wab, poolw, pscale, wpb, wout, sub):
    batch, seq, _ = x.shape
    tile = min(MIXER_TILE, seq)
    ext = tile + 2 * HALO
    per_tile = tile // HALO
    last_halo = seq // HALO - 1
    cos, sin = _rope_tables(seq, tile)
    return pl.pallas_call(
        functools.partial(_mixer_kernel, sub=sub, seq=seq),
        grid=(batch, seq // tile),
        in_specs=[
            pl.BlockSpec((1, tile, D_MODEL), lambda b, i: (b, i, 0)),
            pl.BlockSpec((1, HALO, D_MODEL), lambda b, i: (b, jnp.minimum((i + 1) * per_tile, last_halo), 0)),
            pl.BlockSpec((1, 3 * N_SUBLAYERS, D_MODEL), lambda b, i: (b, 0, 0)),
            _resident((N_SUBLAYERS, D_MODEL)),
            _resident((N_SUBLAYERS, D_MODEL)),
            pl.BlockSpec((1, tile + HALO, V7X_LANES), lambda b, i: (i, 0, 0)),
            pl.BlockSpec((1, tile + HALO, V7X_LANES), lambda b, i: (i, 0, 0)),
            pl.BlockSpec(memory_space=pltpu.SMEM),
            _resident(win.shape),
            _resident(wab.shape),
            _resident(poolw.shape),
            _resident(pscale.shape),
            _resident(wpb.shape),
            _resident(wout.shape),
        ],
        out_specs=pl.BlockSpec((1, tile, D_MODEL), lambda b, i: (b, i, 0)),
        out_shape=jax.ShapeDtypeStruct(x.shape, F32),
        scratch_shapes=[
            pltpu.VMEM((tile + HALO, D_MODEL), BF16),
            pltpu.VMEM((tile, ATTN_WIDTH), BF16),
            pltpu.VMEM((tile, ATTN_WIDTH), BF16),
            pltpu.VMEM((N_KV_HEADS, ext, V7X_LANES), BF16),
            pltpu.VMEM((N_KV_HEADS, ext, V7X_LANES), BF16),
            pltpu.VMEM((ext, POOL_WIDTH), F32),
            pltpu.VMEM((tile, 2 * D_MODEL), F32),
            pltpu.VMEM((tile, ATTN_WIDTH), BF16),
            pltpu.VMEM((tile, POOL_WIDTH), BF16),
            pltpu.VMEM((tile + 8, POOL_GROUP_DIM), F32),
            pltpu.VMEM((tile + 16, POOL_GROUP_DIM), F32),
            pltpu.VMEM((tile + 8, POOL_GROUP_DIM), F32),
            pltpu.VMEM((tile + 24, POOL_GROUP_DIM), F32),
            pltpu.VMEM((tile + 16, POOL_GROUP_DIM), F32),
            pltpu.VMEM((tile + 8, POOL_GROUP_DIM), F32),
        ],
        compiler_params=pltpu.CompilerParams(dimension_semantics=("arbitrary", "arbitrary"),
                                             vmem_limit_bytes=VMEM_LIMIT),
        name=f"mixer_sublayer{sub}",
    )(x, x, mod, norm_pre, norm_post, cos, sin, sink, win, wab, poolw, pscale, wpb, wout)


def _encoder_layer(xs, mods, norm_pre, norm_post, ffn1, mixer, ffn2):
    outs = []
    for x, mod in zip(xs, mods):
        x = _ffn_sublayer(x, mod, norm_pre, norm_post, *ffn1, sub=0)
        x = _mixer_sublayer(x, mod, norm_pre, norm_post, *mixer, sub=1)
        x = _ffn_sublayer(x, mod, norm_pre, norm_post, *ffn2, sub=2)
        outs.append(x)
    return outs


def kernel(x_prompt, x_sample, c_prompt, c_sample, ada_w, ada_b, norm_pre, norm_post, ffn1_w_gate, ffn1_w_up, ffn1_w_down, w_in, attn_sink, w_attn_branch, pool_w, pool_scale, w_pool_branch, w_out, ffn2_w_gate, ffn2_w_up, ffn2_w_down):
    xs = [x_prompt, x_sample]
    n_prompt, n_sample = c_prompt.shape[0], c_sample.shape[0]
    rows = -(-(n_prompt + n_sample) // V7X_SUBLANES) * V7X_SUBLANES
    c_all = jnp.concatenate([c_prompt, c_sample, jnp.zeros((rows - n_prompt - n_sample, D_MODEL), F32)], axis=0)
    for l in range(ada_w.shape[0]):
        mod = _ada_mod(c_all, ada_w[l], ada_b[l]).reshape(rows, 3 * N_SUBLAYERS, D_MODEL)
        mods = [mod[:n_prompt], mod[n_prompt:n_prompt + n_sample]]
        ffn1 = (ffn1_w_gate[l].astype(BF16), ffn1_w_up[l].astype(BF16), ffn1_w_down[l].astype(BF16))
        ffn2 = (ffn2_w_gate[l].astype(BF16), ffn2_w_up[l].astype(BF16), ffn2_w_down[l].astype(BF16))
        mixer = (
            attn_sink[l],
            w_in[l].astype(BF16),
            w_attn_branch[l].astype(BF16),
            pool_w[l].astype(BF16),
            pool_scale[l].reshape(1, POOL_WIDTH),
            w_pool_branch[l].astype(BF16),
            w_out[l].astype(BF16),
        )
        xs = _encoder_layer(xs, mods, norm_pre[l], norm_post[l], ffn1, mixer, ffn2)
    return (xs[0], xs[1])
```

```python
import functools

import jax
import jax.numpy as jnp
from jax import lax
from jax.experimental import pallas as pl
from jax.experimental.pallas import tpu as pltpu

D_MODEL = 1024
N_HEADS = 8
N_KV_HEADS = 2
HEAD_DIM = 64
GROUP = N_HEADS // N_KV_HEADS
ATTN_WIDTH = N_HEADS * HEAD_DIM
KV_WIDTH = N_KV_HEADS * HEAD_DIM
WINDOW = 128
ROPE_THETA = 10000.0
POOL_WINDOWS = (2, 4, 8, 16)
POOL_GROUP_DIM = 128
POOL_WIDTH = len(POOL_WINDOWS) * POOL_GROUP_DIM
D_FF = 2816
N_SUBLAYERS = 3
FFN_RES_WEIGHT = 0.5
EPS = 1e-6
NEG_INF = -1e30
F32_MAX = 3.4028234663852886e38

V7X_LANES = 128
V7X_SUBLANES = 8
V7X_VMEM_BYTES = 64 * 1024 * 1024

HALO = WINDOW
FFN_TILE = 1024
FFN_SUBTILE = 256
MIXER_TILE = 512
ADA_COLS = 1536
VMEM_LIMIT = V7X_VMEM_BYTES - 8 * 1024 * 1024

F32 = jnp.float32
BF16 = jnp.bfloat16


def _sigmoid(x):
    return 1.0 / (1.0 + jnp.exp(-x))


def _rms_scale(x):
    return x * lax.rsqrt(jnp.mean(x * x, axis=-1, keepdims=True) + EPS)


def _resident(shape):
    return pl.BlockSpec(shape, lambda *_: (0,) * len(shape), pipeline_mode=pl.Buffered(1))


def _ada_kernel(c_ref, w_ref, b_ref, o_ref):
    c = c_ref[...]
    s = c * _sigmoid(c)
    w = w_ref[...]
    s_hi = s.astype(BF16)
    s_lo = (s - s_hi.astype(F32)).astype(BF16)
    w_hi = w.astype(BF16)
    w_lo = (w - w_hi.astype(F32)).astype(BF16)
    rows = s.shape[0]
    by_hi = jnp.dot(jnp.concatenate([s_hi, s_lo], axis=0), w_hi, preferred_element_type=F32)
    o_ref[...] = (jnp.dot(s_hi, w_lo, preferred_element_type=F32) + by_hi[rows:]) + by_hi[:rows] + b_ref[...]


def _ada_mod(c, ada_w, ada_b):
    rows, width = c.shape[0], ada_w.shape[1]
    return pl.pallas_call(
        _ada_kernel,
        grid=(width // ADA_COLS,),
        in_specs=[
            pl.BlockSpec((rows, D_MODEL), lambda n: (0, 0)),
            pl.BlockSpec((D_MODEL, ADA_COLS), lambda n: (0, n)),
            pl.BlockSpec((1, ADA_COLS), lambda n: (0, n)),
        ],
        out_specs=pl.BlockSpec((rows, ADA_COLS), lambda n: (0, n)),
        out_shape=jax.ShapeDtypeStruct((rows, width), F32),
        compiler_params=pltpu.CompilerParams(dimension_semantics=("arbitrary",),
                                             vmem_limit_bytes=VMEM_LIMIT),
        name="ada_mod",
    )(c, ada_w, ada_b.reshape(1, width))


def _ffn_kernel(x_ref, mod_ref, gpre_ref, gpost_ref, wg_ref, wu_ref, wd_ref, o_ref, *, sub):
    shift = mod_ref[0, 3 * sub:3 * sub + 1, :]
    scl = mod_ref[0, 3 * sub + 1:3 * sub + 2, :]
    gate = mod_ref[0, 3 * sub + 2:3 * sub + 3, :]
    gain_pre = gpre_ref[sub:sub + 1, :] * (1.0 + scl)
    gain_post = gpost_ref[sub:sub + 1, :] * (FFN_RES_WEIGHT * gate)
    n_sub = x_ref.shape[1] // FFN_SUBTILE

    def rows(k):
        return slice(k * FFN_SUBTILE, (k + 1) * FFN_SUBTILE)

    def pre(k):
        return (_rms_scale(x_ref[0, rows(k), :]) * gain_pre + shift).astype(BF16)

    def post(k, y):
        o_ref[0, rows(k), :] = x_ref[0, rows(k), :] + _rms_scale(y) * gain_post

    h = pre(0)
    y_prev = None
    for k in range(n_sub):
        g = jnp.dot(h, wg_ref[...], preferred_element_type=F32)
        u = jnp.dot(h, wu_ref[...], preferred_element_type=F32)
        if k + 1 < n_sub:
            h = pre(k + 1)
        if y_prev is not None:
            post(k - 1, y_prev)
        a = (g * _sigmoid(g) * u).astype(BF16)
        y_prev = jnp.dot(a, wd_ref[...], preferred_element_type=F32)
    post(n_sub - 1, y_prev)


def _ffn_sublayer(x, mod, norm_pre, norm_post, wg, wu, wd, sub):
    batch, seq, _ = x.shape
    tile = min(FFN_TILE, seq)
    return pl.pallas_call(
        functools.partial(_ffn_kernel, sub=sub),
        grid=(batch, seq // tile),
        in_specs=[
            pl.BlockSpec((1, tile, D_MODEL), lambda b, i: (b, i, 0)),
            pl.BlockSpec((1, 3 * N_SUBLAYERS, D_MODEL), lambda b, i: (b, 0, 0)),
            _resident((N_SUBLAYERS, D_MODEL)),
            _resident((N_SUBLAYERS, D_MODEL)),
            _resident((D_MODEL, D_FF)),
            _resident((D_MODEL, D_FF)),
            _resident((D_FF, D_MODEL)),
        ],
        out_specs=pl.BlockSpec((1, tile, D_MODEL), lambda b, i: (b, i, 0)),
        out_shape=jax.ShapeDtypeStruct(x.shape, F32),
        compiler_params=pltpu.CompilerParams(dimension_semantics=("arbitrary", "arbitrary"),
                                             vmem_limit_bytes=VMEM_LIMIT),
        name=f"ffn_sublayer{sub}",
    )(x, mod, norm_pre, norm_post, wg, wu, wd)


def _rope(x, cos, sin_signed, first_half):
    partner = jnp.where(first_half, pltpu.roll(x, V7X_LANES - HEAD_DIM // 2, 1),
                        pltpu.roll(x, HEAD_DIM // 2, 1))
    return x * cos + partner * sin_signed


def _mixer_kernel(xm_ref, xr_ref, mod_ref, gpre_ref, gpost_ref, cos_ref, sin_ref, sink_ref,
                  win_ref, wab_ref, poolw_ref, pscale_ref, wpb_ref, wout_ref,
                  o_ref,
                  hext_ref, qlo_ref, qhi_ref, k_ref, v_ref, u_ref, g_ref, attn_ref, mixed_ref,
                  c1_ref, c2_ref, b2_ref, c3_ref, b3_ref, a3_ref, *, sub, seq):
    tile = xm_ref.shape[1]
    ext = tile + 2 * HALO
    blk = WINDOW
    n_blk = tile // blk
    i = pl.program_id(1)
    n_tiles = pl.num_programs(1)
    shift = mod_ref[0, 3 * sub:3 * sub + 1, :]
    scl = mod_ref[0, 3 * sub + 1:3 * sub + 2, :]
    gate = mod_ref[0, 3 * sub + 2:3 * sub + 3, :]
    gain_pre = gpre_ref[sub:sub + 1, :] * (1.0 + scl)
    gain_post = gpost_ref[sub:sub + 1, :] * gate
    e_q, e_v, e_u = ATTN_WIDTH, ATTN_WIDTH + 2 * KV_WIDTH, ATTN_WIDTH + 2 * KV_WIDTH + POOL_WIDTH

    def pre(x):
        return (_rms_scale(x) * gain_pre + shift).astype(BF16)

    @pl.when(i == 0)
    def _():
        k_ref[:, 0:HALO, :] = jnp.zeros((N_KV_HEADS, HALO, V7X_LANES), BF16)
        v_ref[:, 0:HALO, :] = jnp.zeros((N_KV_HEADS, HALO, V7X_LANES), BF16)
        u_ref[0:HALO, :] = jnp.zeros((HALO, POOL_WIDTH), F32)

    @pl.when(i > 0)
    def _():
        k_ref[:, 0:HALO, :] = k_ref[:, tile:tile + HALO, :]
        v_ref[:, 0:HALO, :] = v_ref[:, tile:tile + HALO, :]
        u_ref[0:HALO, :] = u_ref[tile:tile + HALO, :]

    hext_ref[0:tile, :] = pre(xm_ref[0])
    hext_ref[tile:tile + HALO, :] = pre(xr_ref[0])

    lane = lax.broadcasted_iota(jnp.int32, (1, V7X_LANES), 1)
    first_half = (lane % HEAD_DIM) < (HEAD_DIM // 2)
    low_head = lane < HEAD_DIM
    cos = cos_ref[0]
    sin = sin_ref[0]

    kv = jnp.dot(hext_ref[...], win_ref[:, e_q:e_v], preferred_element_type=F32)
    k = _rope(kv[:, 0:KV_WIDTH], cos, sin, first_half)
    v = kv[:, KV_WIDTH:2 * KV_WIDTH]
    k_sw = pltpu.roll(k, HEAD_DIM, 1)
    v_sw = pltpu.roll(v, HEAD_DIM, 1)
    k_ref[0, HALO:ext, :] = jnp.where(low_head, k, k_sw).astype(BF16)
    k_ref[1, HALO:ext, :] = jnp.where(low_head, k_sw, k).astype(BF16)
    v_ref[0, HALO:ext, :] = jnp.where(low_head, v, 1.0).astype(BF16)
    v_ref[1, HALO:ext, :] = jnp.where(low_head, v_sw, 1.0).astype(BF16)

    q = jnp.dot(hext_ref[0:tile, :], win_ref[:, 0:e_q], preferred_element_type=F32)
    cos_m = cos[0:tile, :]
    sin_m = sin[0:tile, :]
    for cb in range(ATTN_WIDTH // V7X_LANES):
        cols = slice(cb * V7X_LANES, (cb + 1) * V7X_LANES)
        qc = _rope(q[:, cols], cos_m, sin_m, first_half) * (HEAD_DIM ** -0.5)
        qlo_ref[:, cols] = jnp.where(low_head, qc, 0.0).astype(BF16)
        qhi_ref[:, cols] = jnp.where(low_head, 0.0, qc).astype(BF16)

    u_halo = 4 * V7X_SUBLANES
    u_new = jnp.dot(hext_ref[0:tile + u_halo, :], win_ref[:, e_v:e_u], preferred_element_type=F32)
    u_ref[HALO:HALO + tile, :] = u_new[0:tile, :]
    u_ref[HALO + tile:HALO + tile + u_halo, :] = jnp.where(i < n_tiles - 1, u_new[tile:tile + u_halo, :], 0.0)

    base = HALO - V7X_SUBLANES
    pos = i * tile + lax.broadcasted_iota(jnp.int32, (tile, 1), 0)
    levels = ((), (c1_ref,), (c2_ref, b2_ref), (c3_ref, b3_ref, a3_ref))

    def pool_group(gi):
        w = POOL_WINDOWS[gi]
        cols = slice(gi * POOL_GROUP_DIM, (gi + 1) * POOL_GROUP_DIM)
        if not levels[gi]:
            acc = u_ref[HALO - 1:HALO - 1 + tile, cols] + u_ref[HALO:HALO + tile, cols]
        else:
            n0 = levels[gi][0].shape[0]
            levels[gi][0][...] = u_ref[base:base + n0, cols] + u_ref[base + 1:base + 1 + n0, cols]
            for lv in range(1, len(levels[gi])):
                n = levels[gi][lv].shape[0]
                step = 2 ** lv
                levels[gi][lv][...] = levels[gi][lv - 1][0:n, :] + levels[gi][lv - 1][step:step + n, :]
            top = levels[gi][-1]
            first = V7X_SUBLANES - w // 2
            acc = top[first:first + tile, :] + top[V7X_SUBLANES:V7X_SUBLANES + tile, :]
        cnt = jnp.minimum(pos + w // 2, seq) - jnp.maximum(pos - w // 2, 0)
        pooled = acc / cnt.astype(F32) - u_ref[HALO:HALO + tile, cols]
        mixed = jnp.dot(pooled.astype(BF16), poolw_ref[gi], preferred_element_type=F32)
        mixed_ref[:, cols] = (mixed * pscale_ref[:, cols]).astype(BF16)

    r = lax.broadcasted_iota(jnp.int32, (blk, 3 * blk), 0)
    jj = lax.broadcasted_iota(jnp.int32, (blk, 3 * blk), 1)
    cap_band = jnp.where((jj >= r) & (jj <= r + 2 * WINDOW), F32_MAX, NEG_INF)
    cap_first = jnp.where((jj < blk) & (i == 0), NEG_INF, cap_band)
    cap_last = jnp.where((jj >= 2 * blk) & (i == n_tiles - 1), NEG_INF, cap_band)

    g_cols = 2 * D_MODEL // (n_blk * N_KV_HEADS)
    instances = [(b, kh) for b in range(n_blk) for kh in range(N_KV_HEADS)]

    def scores(b, kh):
        rows = slice(b * blk, (b + 1) * blk)
        q4 = jnp.concatenate(
            [ref[rows, (2 * kh + c) * V7X_LANES:(2 * kh + c + 1) * V7X_LANES]
             for c in range(GROUP // 2) for ref in (qlo_ref, qhi_ref)], axis=0)
        kw = k_ref[kh, b * blk:b * blk + 3 * blk, :]
        return lax.dot_general(q4, kw, (((1,), (1,)), ((), ())), preferred_element_type=F32)

    s_next = scores(*instances[0])
    for n, (b, kh) in enumerate(instances):
        rows = slice(b * blk, (b + 1) * blk)
        cap = cap_band
        if b == 0:
            cap = cap_first
        if b == n_blk - 1:
            cap = jnp.minimum(cap, cap_last)
        s = s_next
        if n + 1 < len(instances):
            s_next = scores(*instances[n + 1])
        vw = v_ref[kh, b * blk:b * blk + 3 * blk, :]
        ps, sink_terms = [], []
        for gi in range(GROUP):
            sink = sink_ref[GROUP * kh + gi]
            sh = s[gi * blk:(gi + 1) * blk, :]
            sg = jnp.concatenate([jnp.minimum(sh[:, :blk], cap[:, :blk]), sh[:, blk:2 * blk],
                                  jnp.minimum(sh[:, 2 * blk:], cap[:, 2 * blk:])], axis=1)
            m = jnp.maximum(jnp.max(sg, axis=-1, keepdims=True), sink)
            ps.append(jnp.exp(sg - m).astype(BF16))
            sink_terms.append(jnp.exp(sink - m))
        o4 = jnp.dot(jnp.concatenate(ps, axis=0), vw, preferred_element_type=F32)
        for c in range(GROUP // 2):
            lo = o4[(2 * c) * blk:(2 * c + 1) * blk, :]
            hi = pltpu.roll(o4[(2 * c + 1) * blk:(2 * c + 2) * blk, :], HEAD_DIM, 1)
            lo = lo / (pltpu.roll(lo, HEAD_DIM, 1) + sink_terms[2 * c])
            hi = hi / (pltpu.roll(hi, HEAD_DIM, 1) + sink_terms[2 * c + 1])
            attn_ref[rows, (2 * kh + c) * V7X_LANES:(2 * kh + c + 1) * V7X_LANES] = (
                jnp.where(low_head, lo, hi).astype(BF16))
        gc = slice(n * g_cols, (n + 1) * g_cols)
        g_ref[:, gc] = jnp.dot(hext_ref[0:tile, :], win_ref[:, e_u + n * g_cols:e_u + (n + 1) * g_cols],
                               preferred_element_type=F32)
        if n % 2 == 1 and n // 2 < len(POOL_WINDOWS):
            pool_group(n // 2)
    for gi in range(len(instances) // 2, len(POOL_WINDOWS)):
        pool_group(gi)

    a = jnp.dot(attn_ref[...], wab_ref[...], preferred_element_type=F32)

    p_branch = jnp.dot(mixed_ref[...], wpb_ref[...], preferred_element_type=F32)

    merged = (_sigmoid(g_ref[:, 0:D_MODEL]) * a + _sigmoid(g_ref[:, D_MODEL:2 * D_MODEL]) * p_branch).astype(BF16)
    y = jnp.dot(merged, wout_ref[...], preferred_element_type=F32)
    o_ref[0] = xm_ref[0] + _rms_scale(y) * gain_post


def _rope_tables(seq, tile):
    half = HEAD_DIM // 2
    inv_freq = ROPE_THETA ** (-jnp.arange(half, dtype=F32) / half)
    pos = (jnp.arange(seq // tile) * tile)[:, None] + jnp.arange(tile + HALO)[None, :]
    ang = pos.astype(F32)[:, :, None] * inv_freq[None, None, :]
    sign = jnp.where((jnp.arange(V7X_LANES) % HEAD_DIM) < half, -1.0, 1.0).astype(F32)
    reps = (1, 1, V7X_LANES // half)
    return jnp.tile(jnp.cos(ang), reps), jnp.tile(jnp.sin(ang), reps) * sign


def _mixer_sublayer(x, mod, norm_pre, norm_post, sink, win, wab, poolw, pscale, wpb, wout, sub):
    batch, seq, _ = x.shape
    tile = min(MIXER_TILE, seq)
    ext = tile + 2 * HALO
    per_tile = tile // HALO
    last_halo = seq // HALO - 1
    cos, sin = _rope_tables(seq, tile)
    return pl.pallas_call(
        functools.partial(_mixer_kernel, sub=sub, seq=seq),
        grid=(batch, seq // tile),
        in_specs=[
            pl.BlockSpec((1, tile, D_MODEL), lambda b, i: (b, i, 0)),
            pl.BlockSpec((1, HALO, D_MODEL), lambda b, i: (b, jnp.minimum((i + 1) * per_tile, last_halo), 0)),
            pl.BlockSpec((1, 3 * N_SUBLAYERS, D_MODEL), lambda b, i: (b, 0, 0)),
            _resident((N_SUBLAYERS, D_MODEL)),
            _resident((N_SUBLAYERS, D_MODEL)),
            pl.BlockSpec((1, tile + HALO, V7X_LANES), lambda b, i: (i, 0, 0)),
            pl.BlockSpec((1, tile + HALO, V7X_LANES), lambda b, i: (i, 0, 0)),
            pl.BlockSpec(memory_space=pltpu.SMEM),
            _resident(win.shape),
            _resident(wab.shape),
            _resident(poolw.shape),
            _resident(pscale.shape),
            _resident(wpb.shape),
            _resident(wout.shape),
        ],
        out_specs=pl.BlockSpec((1, tile, D_MODEL), lambda b, i: (b, i, 0)),
        out_shape=jax.ShapeDtypeStruct(x.shape, F32),
        scratch_shapes=[
            pltpu.VMEM((tile + HALO, D_MODEL), BF16),
            pltpu.VMEM((tile, ATTN_WIDTH), BF16),
            pltpu.VMEM((tile, ATTN_WIDTH), BF16),
            pltpu.VMEM((N_KV_HEADS, ext, V7X_LANES), BF16),
            pltpu.VMEM((N_KV_HEADS, ext, V7X_LANES), BF16),
            pltpu.VMEM((ext, POOL_WIDTH), F32),
            pltpu.VMEM((tile, 2 * D_MODEL), F32),
            pltpu.VMEM((tile, ATTN_WIDTH), BF16),
            pltpu.VMEM((tile, POOL_WIDTH), BF16),
            pltpu.VMEM((tile + 8, POOL_GROUP_DIM), F32),
            pltpu.VMEM((tile + 16, POOL_GROUP_DIM), F32),
            pltpu.VMEM((tile + 8, POOL_GROUP_DIM), F32),
            pltpu.VMEM((tile + 24, POOL_GROUP_DIM), F32),
            pltpu.VMEM((tile + 16, POOL_GROUP_DIM), F32),
            pltpu.VMEM((tile + 8, POOL_GROUP_DIM), F32),
        ],
        compiler_params=pltpu.CompilerParams(dimension_semantics=("arbitrary", "arbitrary"),
                                             vmem_limit_bytes=VMEM_LIMIT),
        name=f"mixer_sublayer{sub}",
    )(x, x, mod, norm_pre, norm_post, cos, sin, sink, win, wab, poolw, pscale, wpb, wout)


def _encoder_layer(xs, mods, norm_pre, norm_post, ffn1, mixer, ffn2):
    outs = []
    for x, mod in zip(xs, mods):
        x = _ffn_sublayer(x, mod, norm_pre, norm_post, *ffn1, sub=0)
        x = _mixer_sublayer(x, mod, norm_pre, norm_post, *mixer, sub=1)
        x = _ffn_sublayer(x, mod, norm_pre, norm_post, *ffn2, sub=2)
        outs.append(x)
    return outs


def kernel(x_prompt, x_sample, c_prompt, c_sample, ada_w, ada_b, norm_pre, norm_post, ffn1_w_gate, ffn1_w_up, ffn1_w_down, w_in, attn_sink, w_attn_branch, pool_w, pool_scale, w_pool_branch, w_out, ffn2_w_gate, ffn2_w_up, ffn2_w_down):
    xs = [x_prompt, x_sample]
    n_prompt, n_sample = c_prompt.shape[0], c_sample.shape[0]
    rows = -(-(n_prompt + n_sample) // V7X_SUBLANES) * V7X_SUBLANES
    c_all = jnp.concatenate([c_prompt, c_sample, jnp.zeros((rows - n_prompt - n_sample, D_MODEL), F32)], axis=0)
    for l in range(ada_w.shape[0]):
        mod = _ada_mod(c_all, ada_w[l], ada_b[l]).reshape(rows, 3 * N_SUBLAYERS, D_MODEL)
        mods = [mod[:n_prompt], mod[n_prompt:n_prompt + n_sample]]
        ffn1 = (ffn1_w_gate[l].astype(BF16), ffn1_w_up[l].astype(BF16), ffn1_w_down[l].astype(BF16))
        ffn2 = (ffn2_w_gate[l].astype(BF16), ffn2_w_up[l].astype(BF16), ffn2_w_down[l].astype(BF16))
        mixer = (
            attn_sink[l],
            w_in[l].astype(BF16),
            w_attn_branch[l].astype(BF16),
            pool_w[l].astype(BF16),
            pool_scale[l].reshape(1, POOL_WIDTH),
            w_pool_branch[l].astype(BF16),
            w_out[l].astype(BF16),
        )
        xs = _encoder_layer(xs, mods, norm_pre[l], norm_post[l], ffn1, mixer, ffn2)
    return (xs[0], xs[1])
```
